```python
import math
import jax, jax.numpy as jnp
from jax import lax
import numpy as np

D_MODEL = 4096
BATCH = 1
SEQ = 8192
DEPTH = 1
DEC_BATCH = 32
DEC_SEQ = 1
PAST_LEN = 8192
PAGE_SIZE = 128

GLA_HEADS = 8
GLA_DK = D_MODEL // (2 * GLA_HEADS)
GLA_DV = D_MODEL // GLA_HEADS
GLA_RANK = 16
GLA_TAU = 16.0
GLA_CHUNK = 64
DIFF_HEADS = 16
DIFF_DH = D_MODEL // (2 * DIFF_HEADS)
DIFF_DV = 2 * DIFF_DH
REL_BUCKETS = 32
REL_MAX_EXACT = REL_BUCKETS // 2
REL_MAX_DIST = 128
N_MEM = 256
MEM_HEADS = 4
MEM_DH = 128
D_FF = 11008
Q_BLOCK = 128
NORM_EPS = 1e-6
NEG_INF = -1e30
IN_SIZES = (GLA_HEADS * GLA_DK, GLA_HEADS * GLA_DK, GLA_HEADS * GLA_DV, GLA_HEADS * GLA_DV, GLA_RANK,
            DIFF_HEADS * 2 * DIFF_DH, DIFF_HEADS * 2 * DIFF_DH, DIFF_HEADS * DIFF_DV)
D_IN = sum(IN_SIZES)

kernel_name = "hybrid_gla_diffattn_macaron_step"


def rmsnorm(x, g):
    x32 = x.astype(jnp.float32)
    y = x32 * lax.rsqrt(jnp.mean(x32 * x32, axis=-1, keepdims=True) + NORM_EPS)
    return (y * g.astype(jnp.float32)).astype(x.dtype)


def swiglu(h, w1, w3, w2):
    return (jax.nn.silu(h @ w1) * (h @ w3)) @ w2


def split_columns(u):
    parts, start = [], 0
    for size in IN_SIZES:
        parts.append(u[..., start:start + size])
        start += size
    return parts


def rel_bias(q_pos, k_pos, table):
    n = jnp.maximum(q_pos[:, None] - k_pos[None, :], 0)
    nf = jnp.maximum(n, 1).astype(jnp.float32)
    large = REL_MAX_EXACT + (jnp.log(nf / REL_MAX_EXACT) / math.log(REL_MAX_DIST / REL_MAX_EXACT)
                             * (REL_BUCKETS - REL_MAX_EXACT)).astype(jnp.int32)
    large = jnp.minimum(large, REL_BUCKETS - 1)
    bucket = jnp.where(n < REL_MAX_EXACT, n, large)
    return jnp.transpose(jnp.take(table, bucket, axis=0), (2, 0, 1)).astype(jnp.float32)


def gla_inputs(q, k, v, a_low, w_alpha, b_alpha):
    B, T = q.shape[:2]
    q = q.reshape(B, T, GLA_HEADS, GLA_DK).astype(jnp.float32) * GLA_DK ** -0.5
    k = k.reshape(B, T, GLA_HEADS, GLA_DK).astype(jnp.float32)
    v = v.reshape(B, T, GLA_HEADS, GLA_DV).astype(jnp.float32)
    la = jax.nn.log_sigmoid((a_low @ w_alpha + b_alpha).astype(jnp.float32)) / GLA_TAU
    return q, k, v, la.reshape(B, T, GLA_HEADS, GLA_DK)


def gla_chunk(S, q, k, v, la):
    C = q.shape[1]
    b = jnp.cumsum(la, axis=1)
    causal = jnp.tril(jnp.ones((C, C), dtype=bool))[None, :, :, None, None]
    diff = b[:, :, None] - b[:, None, :]
    decay = jnp.where(causal, jnp.exp(jnp.minimum(diff, 0.0)), 0.0)
    A = jnp.einsum('bthd,bshd,btshd->bhts', q, k, decay)
    o = jnp.einsum('bhts,bshv->bthv', A, v) + jnp.einsum('bthd,bhdv->bthv', q * jnp.exp(b), S)
    b_last = b[:, -1]
    k_dec = k * jnp.exp(b_last[:, None] - b)
    S_new = jnp.exp(b_last)[..., None] * S + jnp.einsum('bshd,bshv->bhdv', k_dec, v)
    return S_new, o


def gla_prompt(q, k, v, la):
    B, T = q.shape[:2]
    nc = T // GLA_CHUNK

    def to_chunks(a):
        return a.reshape(B, nc, GLA_CHUNK, *a.shape[2:]).swapaxes(0, 1)

    S0 = jnp.zeros((B, GLA_HEADS, GLA_DK, GLA_DV), jnp.float32)
    S, o = lax.scan(lambda S, c: gla_chunk(S, *c), S0, (to_chunks(q), to_chunks(k), to_chunks(v), to_chunks(la)))
    return S, o.swapaxes(0, 1).reshape(B, T, GLA_HEADS, GLA_DV)


def gla_output(o, r, gain, w_br):
    B, T = o.shape[:2]
    gated = rmsnorm(o, gain) * jax.nn.silu(r.astype(jnp.float32)).reshape(B, T, GLA_HEADS, GLA_DV)
    return gated.reshape(B, T, GLA_HEADS * GLA_DV).astype(r.dtype) @ w_br


def diff_core(q, k, v, bias, mask, lam):
    s = jnp.einsum('bqhcd,bkhcd->bchqk', q, k).astype(jnp.float32) * DIFF_DH ** -0.5 + bias[None, None]
    p = jax.nn.softmax(jnp.where(mask, s, NEG_INF), axis=-1)
    a = p[:, 0] - lam * p[:, 1]
    return jnp.einsum('bhqk,bkhv->bqhv', a.astype(v.dtype), v)


def diff_prompt(q, k, v, lam, table):
    B, T = q.shape[:2]
    nb = T // Q_BLOCK
    qb = q.reshape(B, nb, Q_BLOCK, DIFF_HEADS, 2, DIFF_DH).swapaxes(0, 1)
    k_pos = jnp.arange(T)

    def block(args):
        qi, i = args
        q_pos = i * Q_BLOCK + jnp.arange(Q_BLOCK)
        return diff_core(qi, k, v, rel_bias(q_pos, k_pos, table), k_pos[None, :] <= q_pos[:, None], lam)

    out = lax.map(block, (qb, jnp.arange(nb)))
    return out.swapaxes(0, 1).reshape(B, T, DIFF_HEADS, DIFF_DV)


def diff_sample(q, k_new, v_new, cache_k, cache_v, page_table, lam, table):
    S = q.shape[1]
    past = page_table.shape[1] * PAGE_SIZE
    q_pos = past + jnp.arange(S)
    k_pos = jnp.arange(past + S)
    bias = rel_bias(q_pos, k_pos, table)
    mask = k_pos[None, :] <= q_pos[:, None]

    def one(args):
        qi, kn, vn, pages = args
        kp = cache_k[pages].reshape(past, DIFF_HEADS, 2, DIFF_DH).astype(kn.dtype)
        vp = cache_v[pages].reshape(past, DIFF_HEADS, DIFF_DV).astype(vn.dtype)
        kk = jnp.concatenate([kp, kn], axis=0)[None]
        vv = jnp.concatenate([vp, vn], axis=0)[None]
        return diff_core(qi[None], kk, vv, bias, mask, lam)[0]

    return lax.map(one, (q, k_new, v_new, page_table))


def diff_output(o, gain, lam_init, w_br):
    B, T = o.shape[:2]
    o = rmsnorm(o, gain) * (1.0 - lam_init)
    return o.reshape(B, T, DIFF_HEADS * DIFF_DV) @ w_br


def gated_merge(h, a, b, w_gate, b_gate, w_out):
    g = jax.nn.sigmoid((h @ w_gate + b_gate).astype(jnp.float32)).reshape(*h.shape[:-1], 2, D_MODEL)
    m = g[..., 0, :] * a.astype(jnp.float32) + g[..., 1, :] * b.astype(jnp.float32)
    return m.astype(h.dtype) @ w_out


def mem_kv(mem, g_mem, w_mk, w_mv):
    B = mem.shape[0]
    m = rmsnorm(mem, g_mem)
    return ((m @ w_mk).reshape(B, N_MEM, MEM_HEADS, MEM_DH), (m @ w_mv).reshape(B, N_MEM, MEM_HEADS, MEM_DH))


def mem_attend(h, mk, mv, w_mq, w_mo):
    B, T = h.shape[:2]
    q = (h @ w_mq).reshape(B, T, MEM_HEADS, MEM_DH)
    s = jnp.einsum('bqhd,bkhd->bhqk', q, mk.astype(q.dtype)).astype(jnp.float32) * MEM_DH ** -0.5
    p = jax.nn.softmax(s, axis=-1)
    o = jnp.einsum('bhqk,bkhd->bqhd', p.astype(q.dtype), mv.astype(q.dtype))
    return o.reshape(B, T, MEM_HEADS * MEM_DH) @ w_mo


def setup_inputs(seed: int = 0) -> dict:
    key = jax.random.key(seed)
    ks = iter(jax.random.split(key, 64))
    f32 = jnp.float32
    n_pages = PAST_LEN // PAGE_SIZE
    used = DEC_BATCH * n_pages
    n_pool = used + max(1, used // 4)

    def w(shape, fan_in):
        return jax.random.normal(next(ks), shape, f32) * fan_in ** -0.5

    def gain(n):
        return 1.0 + 0.02 * jax.random.normal(next(ks), (n,), f32)

    def small(shape, scale):
        return scale * jax.random.normal(next(ks), shape, f32)

    page_table = jax.random.permutation(next(ks), n_pool)[:used].reshape(DEC_BATCH, n_pages).astype(jnp.int32)
    return {
        "x_prompt": jax.random.normal(next(ks), (BATCH, SEQ, D_MODEL), f32),
        "x_sample": jax.random.normal(next(ks), (DEC_BATCH, DEC_SEQ, D_MODEL), f32),
        "cache_diff_k": jax.random.normal(next(ks), (n_pool, PAGE_SIZE, DIFF_HEADS, 2 * DIFF_DH), f32),
        "cache_diff_v": jax.random.normal(next(ks), (n_pool, PAGE_SIZE, DIFF_HEADS, DIFF_DV), f32),
        "cache_mem_k": jax.random.normal(next(ks), (DEC_BATCH, N_MEM, MEM_HEADS, MEM_DH), f32),
        "cache_mem_v": jax.random.normal(next(ks), (DEC_BATCH, N_MEM, MEM_HEADS, MEM_DH), f32),
        "state_gla": small((DEC_BATCH, GLA_HEADS, GLA_DK, GLA_DV), 0.1),
        "page_table": page_table,
        "mem_prompt": jax.random.normal(next(ks), (BATCH, N_MEM, D_MODEL), f32),
        "rel_bias_table": small((REL_BUCKETS, DIFF_HEADS), 0.5),
        "ffn_a_norm": gain(D_MODEL),
        "ffn_a_w1": w((D_MODEL, D_FF), D_MODEL),
        "ffn_a_w3": w((D_MODEL, D_FF), D_MODEL),
        "ffn_a_w2": w((D_FF, D_MODEL), D_FF),
        "mix_norm": gain(D_MODEL),
        "w_in": w((D_MODEL, D_IN), D_MODEL),
        "gla_w_alpha": w((GLA_RANK, GLA_HEADS * GLA_DK), GLA_RANK),
        "gla_b_alpha": small((GLA_HEADS * GLA_DK,), 0.1),
        "gla_out_norm": gain(GLA_DV),
        "diff_lambda_q1": small((DIFF_DH,), 0.1),
        "diff_lambda_k1": small((DIFF_DH,), 0.1),
        "diff_lambda_q2": small((DIFF_DH,), 0.1),
        "diff_lambda_k2": small((DIFF_DH,), 0.1),
        "diff_subln": gain(DIFF_DV),
        "w_branch_gla": w((GLA_HEADS * GLA_DV, D_MODEL), GLA_HEADS * GLA_DV),
        "w_branch_diff": w((DIFF_HEADS * DIFF_DV, D_MODEL), DIFF_HEADS * DIFF_DV),
        "w_gate": w((D_MODEL, 2 * D_MODEL), D_MODEL),
        "b_gate": small((2 * D_MODEL,), 0.1),
        "w_out": w((D_MODEL, D_MODEL), D_MODEL),
        "cross_norm": gain(D_MODEL),
        "mem_norm": gain(D_MODEL),
        "w_mq": w((D_MODEL, MEM_HEADS * MEM_DH), D_MODEL),
        "w_mk": w((D_MODEL, MEM_HEADS * MEM_DH), D_MODEL),
        "w_mv": w((D_MODEL, MEM_HEADS * MEM_DH), D_MODEL),
        "w_mo": w((MEM_HEADS * MEM_DH, D_MODEL), MEM_HEADS * MEM_DH),
        "ffn_b_norm": gain(D_MODEL),
        "ffn_b_w1": w((D_MODEL, D_FF), D_MODEL),
        "ffn_b_w3": w((D_MODEL, D_FF), D_MODEL),
        "ffn_b_w2": w((D_FF, D_MODEL), D_FF),
        "final_norm": gain(D_MODEL),
    }


def reference(x_prompt, x_sample, cache_diff_k, cache_diff_v, cache_mem_k, cache_mem_v, state_gla, page_table,
              mem_prompt, rel_bias_table,
              ffn_a_norm, ffn_a_w1, ffn_a_w3, ffn_a_w2,
              mix_norm, w_in, gla_w_alpha, gla_b_alpha, gla_out_norm,
              diff_lambda_q1, diff_lambda_k1, diff_lambda_q2, diff_lambda_k2, diff_subln,
              w_branch_gla, w_branch_diff, w_gate, b_gate, w_out,
              cross_norm, mem_norm, w_mq, w_mk, w_mv, w_mo,
              ffn_b_norm, ffn_b_w1, ffn_b_w3, ffn_b_w2,
              final_norm):
    xp, xs = x_prompt, x_sample
    for layer in range(DEPTH):
        lam_init = 0.8 - 0.6 * math.exp(-0.3 * layer)
        f32 = jnp.float32
        lam = (jnp.exp(jnp.sum(diff_lambda_q1.astype(f32) * diff_lambda_k1.astype(f32)))
               - jnp.exp(jnp.sum(diff_lambda_q2.astype(f32) * diff_lambda_k2.astype(f32))) + lam_init)

        xp = xp + 0.5 * swiglu(rmsnorm(xp, ffn_a_norm), ffn_a_w1, ffn_a_w3, ffn_a_w2)
        xs = xs + 0.5 * swiglu(rmsnorm(xs, ffn_a_norm), ffn_a_w1, ffn_a_w3, ffn_a_w2)

        hp = rmsnorm(xp, mix_norm)
        hs = rmsnorm(xs, mix_norm)
        gq_p, gk_p, gv_p, gr_p, ga_p, dq_p, dk_p, dv_p = split_columns(hp @ w_in)
        gq_s, gk_s, gv_s, gr_s, ga_s, dq_s, dk_s, dv_s = split_columns(hs @ w_in)

        q, k, v, la = gla_inputs(gq_p, gk_p, gv_p, ga_p, gla_w_alpha, gla_b_alpha)
        gla_state_prompt, o = gla_prompt(q, k, v, la)
        gla_p = gla_output(o, gr_p, gla_out_norm, w_branch_gla)
        q, k, v, la = gla_inputs(gq_s, gk_s, gv_s, ga_s, gla_w_alpha, gla_b_alpha)
        gla_state_sample, o = gla_chunk(state_gla.astype(f32), q, k, v, la)
        gla_s = gla_output(o, gr_s, gla_out_norm, w_branch_gla)

        Bp, Tp = xp.shape[:2]
        Bs, Ts = xs.shape[:2]
        dq = dq_p.reshape(Bp, Tp, DIFF_HEADS, 2, DIFF_DH)
        dk = dk_p.reshape(Bp, Tp, DIFF_HEADS, 2, DIFF_DH)
        dv = dv_p.reshape(Bp, Tp, DIFF_HEADS, DIFF_DV)
        diff_p = diff_output(diff_prompt(dq, dk, dv, lam, rel_bias_table), diff_subln, lam_init, w_branch_diff)
        diff_k_prompt = dk.reshape(Bp, Tp, DIFF_HEADS, 2 * DIFF_DH)
        diff_v_prompt = dv
        dq = dq_s.reshape(Bs, Ts, DIFF_HEADS, 2, DIFF_DH)
        dk = dk_s.reshape(Bs, Ts, DIFF_HEADS, 2, DIFF_DH)
        dv = dv_s.reshape(Bs, Ts, DIFF_HEADS, DIFF_DV)
        diff_s = diff_output(diff_sample(dq, dk, dv, cache_diff_k, cache_diff_v, page_table, lam, rel_bias_table),
                             diff_subln, lam_init, w_branch_diff)
        diff_k_sample = dk.reshape(Bs, Ts, DIFF_HEADS, 2 * DIFF_DH)
        diff_v_sample = dv

        xp = xp + gated_merge(hp, gla_p, diff_p, w_gate, b_gate, w_out)
        xs = xs + gated_merge(hs, gla_s, diff_s, w_gate, b_gate, w_out)

        mem_k_prompt, mem_v_prompt = mem_kv(mem_prompt, mem_norm, w_mk, w_mv)
        xp = xp + mem_attend(rmsnorm(xp, cross_norm), mem_k_prompt, mem_v_prompt, w_mq, w_mo)
        xs = xs + mem_attend(rmsnorm(xs, cross_norm), cache_mem_k, cache_mem_v, w_mq, w_mo)

        xp = xp + 0.5 * swiglu(rmsnorm(xp, ffn_b_norm), ffn_b_w1, ffn_b_w3, ffn_b_w2)
        xs = xs + 0.5 * swiglu(rmsnorm(xs, ffn_b_norm), ffn_b_w1, ffn_b_w3, ffn_b_w2)

    y_prompt = rmsnorm(xp, final_norm)
    y_sample = rmsnorm(xs, final_norm)
    return (y_prompt, y_sample, diff_k_prompt, diff_v_prompt, diff_k_sample, diff_v_sample,
            gla_state_prompt, gla_state_sample, mem_k_prompt, mem_v_prompt)
```

```python
import functools
import math

import numpy as np
import jax
import jax.numpy as jnp
from jax import lax
from jax.experimental import pallas as pl
from jax.experimental.pallas import tpu as pltpu

F32 = jnp.float32
BF16 = jnp.bfloat16

NORM_EPS = 1e-6
NEG_INF = -1e30
GLA_TAU = 16.0
GLA_HEADS = 8
GLA_RANK = 16
GLA_CHUNK = 64
GLA_SUB = 16
DIFF_HEADS = 16
DIFF_DH = 128
DIFF_DV = 256
REL_BUCKETS = 32
REL_MAX_EXACT = 16
REL_MAX_DIST = 128
MEM_HEADS = 4
MEM_DH = 128
PAGE_SIZE = 128
ATT_TILE = 512
LANES = 128
V7X_VMEM_LIMIT = 56 * 1024 * 1024

NT_DIMS = (((1,), (1,)), ((), ()))
TN_DIMS = (((0,), (0,)), ((), ()))


def _params(*semantics):
    return pltpu.CompilerParams(dimension_semantics=semantics, vmem_limit_bytes=V7X_VMEM_LIMIT)


def _silu(x):
    return x / (1.0 + jnp.exp(-x))


def _sigmoid(x):
    return 1.0 / (1.0 + jnp.exp(-x))


def _rmsnorm_kernel(x_ref, g_ref, o_ref):
    x = x_ref[...]
    ms = jnp.mean(x * x, axis=-1, keepdims=True)
    o_ref[...] = (x * lax.rsqrt(ms + NORM_EPS) * g_ref[...]).astype(o_ref.dtype)


def _rmsnorm(x, g, out_dtype):
    m, d = x.shape
    bm = min(m, 256)
    return pl.pallas_call(
        _rmsnorm_kernel,
        grid=(m // bm,),
        in_specs=[pl.BlockSpec((bm, d), lambda i: (i, 0)), pl.BlockSpec((1, d), lambda i: (0, 0))],
        out_specs=pl.BlockSpec((bm, d), lambda i: (i, 0)),
        out_shape=jax.ShapeDtypeStruct((m, d), out_dtype),
        compiler_params=_params("parallel"),
        name="rmsnorm",
    )(x, g.reshape(1, d))


def _mm_kernel(*refs, n_w, epilogue):
    x_ref, w_refs, e_refs, o_ref = refs[0], refs[1:1 + n_w], refs[1 + n_w:-1], refs[-1]
    x = x_ref[...]
    accs = [jnp.dot(x, w[...].astype(BF16), preferred_element_type=F32) for w in w_refs]
    o_ref[...] = epilogue(accs, [e[...] for e in e_refs]).astype(o_ref.dtype)


def _matmul(x, weights, n_out, out_dtype, epilogue, extras=(), name="matmul"):
    m, k = x.shape
    bn = min(n_out, 256)
    if m <= 1024:
        bm = m
    else:
        bm = 1024 if k <= 4096 else 512
    x_mode = {} if k <= 4096 else dict(pipeline_mode=pl.Buffered(1))
    in_specs = [pl.BlockSpec((bm, k), lambda i, j: (i, 0), **x_mode)]
    args = [x]
    for w, off in weights:
        in_specs.append(pl.BlockSpec((k, bn), functools.partial(lambda i, j, o: (0, o + j), o=off // bn)))
        args.append(w)
    for e, off in extras:
        if e.shape[0] == 1:
            in_specs.append(pl.BlockSpec((1, bn), functools.partial(lambda i, j, o: (0, o + j), o=off // bn)))
        else:
            in_specs.append(pl.BlockSpec((bm, bn), functools.partial(lambda i, j, o: (i, o + j), o=off // bn)))
        args.append(e)
    return pl.pallas_call(
        functools.partial(_mm_kernel, n_w=len(weights), epilogue=epilogue),
        grid=(m // bm, n_out // bn),
        in_specs=in_specs,
        out_specs=pl.BlockSpec((bm, bn), lambda i, j: (i, j)),
        out_shape=jax.ShapeDtypeStruct((m, n_out), out_dtype),
        compiler_params=_params("parallel", "arbitrary"),
        name=name,
    )(*args)


def _ep_plain(accs, extras):
    return accs[0]


def _ep_swiglu(accs, extras):
    return _silu(accs[0]) * accs[1]


def _ep_residual(accs, extras, *, scale):
    return extras[0] + scale * accs[0]


def _ep_log_decay(accs, extras):
    z = accs[0] + extras[0]
    return (jnp.minimum(z, 0.0) - jnp.log(1.0 + jnp.exp(-jnp.abs(z)))) * (1.0 / GLA_TAU)


def _ep_gate(accs, extras):
    bias_a, bias_b, a, b = extras
    return _sigmoid(accs[0] + bias_a) * a + _sigmoid(accs[1] + bias_b) * b


def _gla_prompt_kernel(q_ref, k_ref, la_ref, v_ref, r_ref, gain_ref, o_ref, st_ref, state, *, n_chunks):
    c = pl.program_id(1)
    chunk, dk = q_ref.shape

    @pl.when(c == 0)
    def _():
        state[...] = jnp.zeros_like(state)

    la = la_ref[...]
    row = lax.broadcasted_iota(jnp.int32, (chunk, chunk), 0)
    col = lax.broadcasted_iota(jnp.int32, (chunk, chunk), 1)
    tril = (col <= row).astype(F32)
    b = jnp.dot(tril, la, precision=lax.Precision.HIGHEST, preferred_element_type=F32)
    q = q_ref[...] * (dk ** -0.5)
    k = k_ref[...]
    vb = v_ref[...].astype(BF16)
    s_t = state[...]

    o = lax.dot_general((q * jnp.exp(b)).astype(BF16), s_t.astype(BF16), NT_DIMS, preferred_element_type=F32)

    key_row = lax.broadcasted_iota(jnp.int32, (chunk, 1), 0)
    lane = lax.broadcasted_iota(jnp.int32, (GLA_SUB, chunk), 1)
    sub_row = lax.broadcasted_iota(jnp.int32, (GLA_SUB, chunk), 0)
    row_blocks = []
    for i in range(chunk // GLA_SUB):
        lo = i * GLA_SUB
        b_i, q_i, k_i = b[lo:lo + GLA_SUB], q[lo:lo + GLA_SUB], k[lo:lo + GLA_SUB]
        if i == 0:
            a_rows = jnp.zeros((GLA_SUB, chunk), F32)
        else:
            ref_b = b[lo - 1:lo]
            q_t = (q_i * jnp.exp(b_i - ref_b)).astype(BF16)
            k_t = jnp.where(key_row < lo, k * jnp.exp(jnp.minimum(ref_b - b, 0.0)), 0.0).astype(BF16)
            a_rows = lax.dot_general(q_t, k_t, NT_DIMS, preferred_element_type=F32)
        diag_col = jnp.where(lane - lo <= sub_row, lane, -1)
        for s in range(GLA_SUB):
            w = jnp.exp(jnp.minimum(b_i - b_i[s:s + 1], 0.0)) * q_i * k_i[s:s + 1]
            a_rows = jnp.where(diag_col == lo + s, jnp.sum(w, axis=-1, keepdims=True), a_rows)
        row_blocks.append(a_rows)
    a = jnp.concatenate(row_blocks, axis=0)
    o = o + jnp.dot(a.astype(BF16), vb, preferred_element_type=F32)

    b_last = b[chunk - 1:chunk]
    k_dec = (k * jnp.exp(b_last - b)).astype(BF16)
    s_new = s_t * jnp.exp(b_last) + lax.dot_general(vb, k_dec, TN_DIMS, preferred_element_type=F32)
    state[...] = s_new

    @pl.when(c == n_chunks - 1)
    def _():
        st_ref[0] = s_new.T

    ms = jnp.mean(o * o, axis=-1, keepdims=True)
    o_ref[...] = (o * lax.rsqrt(ms + NORM_EPS) * gain_ref[...] * _silu(r_ref[...])).astype(o_ref.dtype)


def _gla_prompt(proj, la, gain):
    t = proj.shape[0]
    dk = la.shape[1] // GLA_HEADS
    dv = 2 * dk
    n_chunks = t // GLA_CHUNK
    h = GLA_HEADS
    return pl.pallas_call(
        functools.partial(_gla_prompt_kernel, n_chunks=n_chunks),
        grid=(h, n_chunks),
        in_specs=[
            pl.BlockSpec((GLA_CHUNK, dk), lambda i, c: (c, i)),
            pl.BlockSpec((GLA_CHUNK, dk), lambda i, c: (c, h + i)),
            pl.BlockSpec((GLA_CHUNK, dk), lambda i, c: (c, i)),
            pl.BlockSpec((GLA_CHUNK, dv), lambda i, c: (c, h + i)),
            pl.BlockSpec((GLA_CHUNK, dv), lambda i, c: (c, 2 * h + i)),
            pl.BlockSpec((1, dv), lambda i, c: (0, 0)),
        ],
        out_specs=[
            pl.BlockSpec((GLA_CHUNK, dv), lambda i, c: (c, i)),
            pl.BlockSpec((1, dk, dv), lambda i, c: (i, 0, 0)),
        ],
        out_shape=[jax.ShapeDtypeStruct((t, h * dv), BF16), jax.ShapeDtypeStruct((h, dk, dv), F32)],
        scratch_shapes=[pltpu.VMEM((dv, dk), F32)],
        compiler_params=_params("parallel", "arbitrary"),
        name="gla_prompt",
    )(proj, proj, la, proj, proj, gain.reshape(1, dv))


def _gla_step_kernel(s_ref, q_ref, k_ref, la_ref, v_ref, r_ref, gain_ref, o_ref, sn_ref):
    heads, dk = s_ref.shape[1], s_ref.shape[2]
    for h in range(heads):
        s_new = jnp.exp(la_ref[0, h]) * s_ref[0, h] + k_ref[0, h] * v_ref[0, h]
        sn_ref[0, h] = s_new
        o = jnp.sum((q_ref[0, h] * (dk ** -0.5)) * s_new, axis=0, keepdims=True)
        ms = jnp.mean(o * o, axis=-1, keepdims=True)
        o_ref[0, h] = (o * lax.rsqrt(ms + NORM_EPS) * gain_ref[...] * _silu(r_ref[0, h])).astype(o_ref.dtype)


def _gla_step(state, q, k, la, v, r, gain):
    bsz, h, dk, dv = state.shape
    col = lambda a: a.reshape(bsz, h, dk, 1)
    row = lambda a: a.reshape(bsz, h, 1, dv)
    col_spec = pl.BlockSpec((1, h, dk, 1), lambda i: (i, 0, 0, 0))
    row_spec = pl.BlockSpec((1, h, 1, dv), lambda i: (i, 0, 0, 0))
    state_spec = pl.BlockSpec((1, h, dk, dv), lambda i: (i, 0, 0, 0))
    out, new_state = pl.pallas_call(
        _gla_step_kernel,
        grid=(bsz,),
        in_specs=[state_spec, col_spec, col_spec, col_spec, row_spec, row_spec, pl.BlockSpec((1, dv), lambda i: (0, 0))],
        out_specs=[row_spec, state_spec],
        out_shape=[jax.ShapeDtypeStruct((bsz, h, 1, dv), BF16), jax.ShapeDtypeStruct(state.shape, F32)],
        compiler_params=_params("parallel"),
        name="gla_step",
    )(state, col(q), col(k), col(la), row(v), row(r), gain.reshape(1, dv))
    return out.reshape(bsz, h * dv), new_state


def _rel_bucket(n):
    n = np.asarray(n, np.int64)
    nf = np.maximum(n, 1).astype(np.float64)
    large = REL_MAX_EXACT + (np.log(nf / REL_MAX_EXACT) / math.log(REL_MAX_DIST / REL_MAX_EXACT)
                             * (REL_BUCKETS - REL_MAX_EXACT)).astype(np.int64)
    return np.where(n < REL_MAX_EXACT, n, np.minimum(large, REL_BUCKETS - 1)).astype(np.int32)


def _bias_tiles_kernel(table_ref, idx_ref, o_ref):
    h = pl.program_id(0)
    for v in range(idx_ref.shape[0]):
        idx = idx_ref[v]
        acc = jnp.where(idx < 0, NEG_INF, 0.0).astype(F32)
        for bkt in range(REL_BUCKETS):
            acc = jnp.where(idx == bkt, table_ref[bkt, h], acc)
        o_ref[0, v] = acc


def _bias_tiles(table, tile):
    r = np.arange(tile)[:, None]
    c = np.arange(tile)[None, :]
    diag = np.where(c <= r, _rel_bucket(np.maximum(r - c, 0)), -1)
    idx = np.stack([diag, _rel_bucket(tile + r - c), _rel_bucket(2 * tile + r - c)]).astype(np.int32)
    heads = table.shape[1]
    return pl.pallas_call(
        _bias_tiles_kernel,
        grid=(heads,),
        in_specs=[pl.BlockSpec(memory_space=pltpu.SMEM), pl.BlockSpec((3, tile, tile), lambda h: (0, 0, 0))],
        out_specs=pl.BlockSpec((1, 3, tile, tile), lambda h: (h, 0, 0, 0)),
        out_shape=jax.ShapeDtypeStruct((heads, 3, tile, tile), F32),
        compiler_params=_params("parallel"),
        name="bias_tiles",
    )(table, jnp.asarray(idx))


def _scalars_kernel(table_ref, onehot_ref, q1_ref, k1_ref, q2_ref, k2_ref, bias_ref, lam_ref, *, lam_init):
    bias_ref[...] = jnp.dot(onehot_ref[...], table_ref[...], precision=lax.Precision.HIGHEST,
                            preferred_element_type=F32)
    s1 = jnp.sum(q1_ref[...] * k1_ref[...], axis=-1, keepdims=True)
    s2 = jnp.sum(q2_ref[...] * k2_ref[...], axis=-1, keepdims=True)
    lam_ref[...] = jnp.broadcast_to(jnp.exp(s1) - jnp.exp(s2) + lam_init, lam_ref.shape)


def _scalars(table, lq1, lk1, lq2, lk2, lam_init, n_dist):
    onehot = np.zeros((n_dist, REL_BUCKETS), np.float32)
    onehot[np.arange(n_dist), _rel_bucket(np.arange(n_dist))] = 1.0
    vec = lambda a: a.reshape(1, -1)
    return pl.pallas_call(
        functools.partial(_scalars_kernel, lam_init=lam_init),
        out_shape=[jax.ShapeDtypeStruct((n_dist, table.shape[1]), F32), jax.ShapeDtypeStruct((1, LANES), F32)],
        name="bias_lambda",
    )(table, jnp.asarray(onehot), vec(lq1), vec(lk1), vec(lq2), vec(lk2))


def _diff_finalize(acc1, l1, acc2, l2, lam, gain, post_scale):
    o = acc1 / l1 - lam * (acc2 / l2)
    ms = jnp.mean(o * o, axis=-1, keepdims=True)
    return o * lax.rsqrt(ms + NORM_EPS) * gain * post_scale


def _flash_kernel(qi_ref, kj_ref, q_ref, k_ref, v_ref, bias_ref, lam_ref, gain_ref, o_ref,
                  m_ref, l_ref, acc_ref, *, post_scale):
    step = pl.program_id(1)
    qi, kj = qi_ref[step], kj_ref[step]
    tq, tk = q_ref.shape[0], k_ref.shape[0]

    @pl.when(kj == 0)
    def _():
        m_ref[...] = jnp.full_like(m_ref, NEG_INF)
        l_ref[...] = jnp.zeros_like(l_ref)
        acc_ref[...] = jnp.zeros_like(acc_ref)

    bias = bias_ref[0, jnp.minimum(qi - kj, 2)]
    vb = v_ref[...].astype(BF16)
    for c in range(2):
        q_c = (q_ref[:, c * DIFF_DH:(c + 1) * DIFF_DH] * (DIFF_DH ** -0.5)).astype(BF16)
        k_c = k_ref[:, c * DIFF_DH:(c + 1) * DIFF_DH].astype(BF16)
        s = lax.dot_general(q_c, k_c, NT_DIMS, preferred_element_type=F32) + bias
        m_prev = m_ref[c]
        m_new = jnp.maximum(m_prev, jnp.max(s, axis=-1, keepdims=True))
        alpha = jnp.exp(m_prev - m_new)
        p = jnp.exp(s - pltpu.repeat(m_new, tk // LANES, axis=1))
        l_ref[c] = alpha * l_ref[c] + jnp.sum(p, axis=-1, keepdims=True)
        m_ref[c] = m_new
        acc_ref[c] = (acc_ref[c] * pltpu.repeat(alpha, DIFF_DV // LANES, axis=1)
                      + jnp.dot(p.astype(BF16), vb, preferred_element_type=F32))

    @pl.when(kj == qi)
    def _():
        rep = lambda a: pltpu.repeat(a, DIFF_DV // LANES, axis=1)
        lam = jnp.concatenate([lam_ref[...]] * (DIFF_DV // LANES), axis=1)
        o_ref[...] = _diff_finalize(acc_ref[0], rep(l_ref[0]), acc_ref[1], rep(l_ref[1]), lam, gain_ref[...],
                                    post_scale).astype(o_ref.dtype)


def _diff_prompt(q, k, v, bias_tiles, lam, gain, post_scale):
    t = q.shape[0]
    tile = bias_tiles.shape[-1]
    n_tiles = t // tile
    pairs = [(i, j) for i in range(n_tiles) for j in range(i + 1)]
    qi_tab = jnp.asarray(np.array([p[0] for p in pairs], np.int32))
    kj_tab = jnp.asarray(np.array([p[1] for p in pairs], np.int32))
    grid_spec = pltpu.PrefetchScalarGridSpec(
        num_scalar_prefetch=2,
        grid=(DIFF_HEADS, len(pairs)),
        in_specs=[
            pl.BlockSpec((tile, DIFF_DV), lambda h, s, qi, kj: (qi[s], h)),
            pl.BlockSpec((tile, DIFF_DV), lambda h, s, qi, kj: (kj[s], h)),
            pl.BlockSpec((tile, DIFF_DV), lambda h, s, qi, kj: (kj[s], h)),
            pl.BlockSpec((1, 3, tile, tile), lambda h, s, qi, kj: (h, 0, 0, 0)),
            pl.BlockSpec((1, LANES), lambda h, s, qi, kj: (0, 0)),
            pl.BlockSpec((1, DIFF_DV), lambda h, s, qi, kj: (0, 0)),
        ],
        out_specs=pl.BlockSpec((tile, DIFF_DV), lambda h, s, qi, kj: (qi[s], h)),
        scratch_shapes=[pltpu.VMEM((2, tile, LANES), F32), pltpu.VMEM((2, tile, LANES), F32),
                        pltpu.VMEM((2, tile, DIFF_DV), F32)],
    )
    return pl.pallas_call(
        functools.partial(_flash_kernel, post_scale=post_scale),
        grid_spec=grid_spec,
        out_shape=jax.ShapeDtypeStruct((t, DIFF_HEADS * DIFF_DV), BF16),
        compiler_params=_params("parallel", "arbitrary"),
        name="diff_attn_prompt",
    )(qi_tab, kj_tab, q, k, v, bias_tiles, lam, gain.reshape(1, DIFF_DV))


def _decode_kernel(pt_ref, q_ref, kn_ref, vn_ref, k_ref, v_ref, bias_ref, bias_new_ref, lam_ref, gain_ref, o_ref,
                   m_ref, l_ref, acc_ref, *, n_pages, post_scale):
    page = pl.program_id(1)

    @pl.when(page == 0)
    def _():
        m_ref[...] = jnp.full_like(m_ref, NEG_INF)
        l_ref[...] = jnp.zeros_like(l_ref)
        acc_ref[...] = jnp.zeros_like(acc_ref)

    q = q_ref[0] * (DIFF_DH ** -0.5)

    def absorb(keys, values, bias):
        prod = keys * q[None]
        for c in range(2):
            s = jnp.sum(prod[:, :, c * DIFF_DH:(c + 1) * DIFF_DH], axis=-1, keepdims=True) + bias
            m_prev = m_ref[c]
            m_new = jnp.maximum(m_prev, jnp.max(s, axis=0))
            alpha = jnp.exp(m_prev - m_new)
            p = jnp.exp(s - m_new[None])
            l_ref[c] = alpha * l_ref[c] + jnp.sum(p, axis=0)
            m_ref[c] = m_new
            weights = jnp.concatenate([p] * (DIFF_DV // LANES), axis=-1)
            acc_ref[c] = (acc_ref[c] * jnp.concatenate([alpha] * (DIFF_DV // LANES), axis=-1)
                          + jnp.sum(weights * values, axis=0))

    absorb(k_ref[0], v_ref[0], bias_ref[0])

    @pl.when(page == n_pages - 1)
    def _():
        absorb(kn_ref[...], vn_ref[...], bias_new_ref[...])
        rep = lambda a: jnp.concatenate([a] * (DIFF_DV // LANES), axis=-1)
        o_ref[0] = _diff_finalize(acc_ref[0], rep(l_ref[0]), acc_ref[1], rep(l_ref[1]), rep(lam_ref[...]),
                                  gain_ref[...], post_scale).astype(o_ref.dtype)


def _diff_decode(q, k_new, v_new, cache_k, cache_v, page_table, bias_pages, bias_new, lam, gain, post_scale):
    bsz, n_pages = page_table.shape
    blk = (1, DIFF_HEADS, DIFF_DV)
    tok_spec = pl.BlockSpec(blk, lambda b, p, pt: (b, 0, 0))
    page_spec = pl.BlockSpec((1, PAGE_SIZE, DIFF_HEADS, DIFF_DV), lambda b, p, pt: (pt[b, p], 0, 0, 0))
    grid_spec = pltpu.PrefetchScalarGridSpec(
        num_scalar_prefetch=1,
        grid=(bsz, n_pages),
        in_specs=[
            tok_spec, tok_spec, tok_spec, page_spec, page_spec,
            pl.BlockSpec((1, PAGE_SIZE, DIFF_HEADS, LANES), lambda b, p, pt: (p // (n_pages - 1), 0, 0, 0)),
            pl.BlockSpec((1, DIFF_HEADS, LANES), lambda b, p, pt: (0, 0, 0)),
            pl.BlockSpec((1, LANES), lambda b, p, pt: (0, 0)),
            pl.BlockSpec((1, DIFF_DV), lambda b, p, pt: (0, 0)),
        ],
        out_specs=tok_spec,
        scratch_shapes=[pltpu.VMEM((2, DIFF_HEADS, LANES), F32), pltpu.VMEM((2, DIFF_HEADS, LANES), F32),
                        pltpu.VMEM((2, DIFF_HEADS, DIFF_DV), F32)],
    )
    return pl.pallas_call(
        functools.partial(_decode_kernel, n_pages=n_pages, post_scale=post_scale),
        grid_spec=grid_spec,
        out_shape=jax.ShapeDtypeStruct((bsz, DIFF_HEADS, DIFF_DV), BF16),
        compiler_params=_params("parallel", "arbitrary"),
        name="diff_attn_decode",
    )(page_table, q, k_new, v_new, cache_k, cache_v, bias_pages, bias_new, lam, gain.reshape(1, DIFF_DV))


def _mem_attn_kernel(q_ref, k_ref, v_ref, o_ref):
    outs = []
    for h in range(MEM_HEADS):
        sl = slice(h * MEM_DH, (h + 1) * MEM_DH)
        s = lax.dot_general(q_ref[0, :, sl], k_ref[0, :, sl].astype(BF16), NT_DIMS,
                            preferred_element_type=F32) * (MEM_DH ** -0.5)
        p = jnp.exp(s - jnp.max(s, axis=-1, keepdims=True))
        p = p / jnp.sum(p, axis=-1, keepdims=True)
        outs.append(jnp.dot(p.astype(BF16), v_ref[0, :, sl].astype(BF16), preferred_element_type=F32))
    o_ref[0] = jnp.concatenate(outs, axis=-1).astype(o_ref.dtype)


def _mem_attn(q, mem_k, mem_v):
    bsz, t, d = q.shape
    n_mem = mem_k.shape[1]
    bt = min(t, 512)
    return pl.pallas_call(
        _mem_attn_kernel,
        grid=(bsz, t // bt),
        in_specs=[pl.BlockSpec((1, bt, d), lambda b, i: (b, i, 0)),
                  pl.BlockSpec((1, n_mem, d), lambda b, i: (b, 0, 0)),
                  pl.BlockSpec((1, n_mem, d), lambda b, i: (b, 0, 0))],
        out_specs=pl.BlockSpec((1, bt, d), lambda b, i: (b, i, 0)),
        out_shape=jax.ShapeDtypeStruct((bsz, t, d), BF16),
        compiler_params=_params("parallel", "parallel"),
        name="mem_attn",
    )(q, mem_k, mem_v)


def _swiglu_half_step(x, norm_g, w1, w3, w2):
    h = _rmsnorm(x, norm_g, BF16)
    d_ff = w1.shape[1]
    gu = _matmul(h, [(w1, 0), (w3, 0)], d_ff, BF16, _ep_swiglu, name="ffn_up")
    return _matmul(gu, [(w2, 0)], x.shape[1], F32, functools.partial(_ep_residual, scale=0.5), [(x, 0)],
                   name="ffn_down")


def kernel(x_prompt, x_sample, cache_diff_k, cache_diff_v, cache_mem_k, cache_mem_v, state_gla, page_table, mem_prompt, rel_bias_table, ffn_a_norm, ffn_a_w1, ffn_a_w3, ffn_a_w2, mix_norm, w_in, gla_w_alpha, gla_b_alpha, gla_out_norm, diff_lambda_q1, diff_lambda_k1, diff_lambda_q2, diff_lambda_k2, diff_subln, w_branch_gla, w_branch_diff, w_gate, b_gate, w_out, cross_norm, mem_norm, w_mq, w_mk, w_mv, w_mo, ffn_b_norm, ffn_b_w1, ffn_b_w3, ffn_b_w2, final_norm):
    t, d = x_prompt.shape[1], x_prompt.shape[2]
    bsz = x_sample.shape[0]
    n_mem = mem_prompt.shape[1]
    d_mem = MEM_HEADS * MEM_DH
    d_gk = gla_w_alpha.shape[1]
    d_gla = 2 * d_gk + 2 * d
    d_diff = DIFF_HEADS * DIFF_DV
    lam_init = 0.8 - 0.6 * math.exp(-0.3 * 0)
    post_scale = 1.0 - lam_init

    xp = x_prompt.reshape(t, d)
    xs = x_sample.reshape(bsz, d)

    w_alpha_in = jnp.pad(w_in[:, d_gla:d_gla + GLA_RANK], ((0, 0), (0, LANES - GLA_RANK)))
    w_diff = w_in[:, d_gla + GLA_RANK:]
    w_alpha = jnp.pad(gla_w_alpha, ((0, LANES - GLA_RANK), (0, 0)))
    b_alpha = gla_b_alpha.reshape(1, d_gk)
    b_gate2 = b_gate.reshape(1, 2 * d)

    bias_by_dist, lam = _scalars(rel_bias_table, diff_lambda_q1, diff_lambda_k1, diff_lambda_q2, diff_lambda_k2,
                                 lam_init, 2 * PAGE_SIZE)
    bias_tiles = _bias_tiles(rel_bias_table, ATT_TILE)

    def mix_inputs(x):
        h = _rmsnorm(x, mix_norm, BF16)
        gla_proj = _matmul(h, [(w_in, 0)], d_gla, F32, _ep_plain, name="proj_gla")
        a_low = _matmul(h, [(w_alpha_in, 0)], LANES, BF16, _ep_plain, name="proj_alpha")
        la = _matmul(a_low, [(w_alpha, 0)], d_gk, F32, _ep_log_decay, [(b_alpha, 0)], name="log_decay")
        dq = _matmul(h, [(w_diff, 0)], d_diff, F32, _ep_plain, name="proj_dq")
        dk = _matmul(h, [(w_diff, d_diff)], d_diff, F32, _ep_plain, name="proj_dk")
        dv = _matmul(h, [(w_diff, 2 * d_diff)], d_diff, F32, _ep_plain, name="proj_dv")
        return h, gla_proj, la, dq, dk, dv

    def merge_and_rest(x, h, gla_gated, diff_normed, mem_k, mem_v):
        gla_out = _matmul(gla_gated, [(w_branch_gla, 0)], d, F32, _ep_plain, name="branch_gla")
        diff_out = _matmul(diff_normed, [(w_branch_diff, 0)], d, F32, _ep_plain, name="branch_diff")
        merged = _matmul(h, [(w_gate, 0), (w_gate, d)], d, BF16, _ep_gate,
                         [(b_gate2, 0), (b_gate2, d), (gla_out, 0), (diff_out, 0)], name="gate_merge")
        x = _matmul(merged, [(w_out, 0)], d, F32, functools.partial(_ep_residual, scale=1.0), [(x, 0)],
                    name="merge_out")
        hc = _rmsnorm(x, cross_norm, BF16)
        qm = _matmul(hc, [(w_mq, 0)], d_mem, BF16, _ep_plain, name="mem_q")
        rows = qm.shape[0] // mem_k.shape[0]
        qm = qm.reshape(mem_k.shape[0], rows, d_mem)
        pad = (-rows) % 8
        om = _mem_attn(jnp.pad(qm, ((0, 0), (0, pad), (0, 0))), mem_k, mem_v)[:, :rows].reshape(-1, d_mem)
        x = _matmul(om, [(w_mo, 0)], d, F32, functools.partial(_ep_residual, scale=1.0), [(x, 0)], name="mem_out")
        x = _swiglu_half_step(x, ffn_b_norm, ffn_b_w1, ffn_b_w3, ffn_b_w2)
        return _rmsnorm(x, final_norm, F32)

    xp = _swiglu_half_step(xp, ffn_a_norm, ffn_a_w1, ffn_a_w3, ffn_a_w2)
    hp, gla_proj_p, la_p, dq_p, dk_p, dv_p = mix_inputs(xp)
    gla_gated_p, gla_state_prompt = _gla_prompt(gla_proj_p, la_p, gla_out_norm)
    diff_normed_p = _diff_prompt(dq_p, dk_p, dv_p, bias_tiles, lam, diff_subln, post_scale)
    mem_normed = _rmsnorm(mem_prompt.reshape(n_mem, d), mem_norm, BF16)
    mem_k_p = _matmul(mem_normed, [(w_mk, 0)], d_mem, F32, _ep_plain, name="mem_k")
    mem_v_p = _matmul(mem_normed, [(w_mv, 0)], d_mem, F32, _ep_plain, name="mem_v")
    y_prompt = merge_and_rest(xp, hp, gla_gated_p, diff_normed_p, mem_k_p[None], mem_v_p[None])

    xs = _swiglu_half_step(xs, ffn_a_norm, ffn_a_w1, ffn_a_w3, ffn_a_w2)
    hs, gla_proj_s, la_s, dq_s, dk_s, dv_s = mix_inputs(xs)
    gla_gated_s, gla_state_sample = _gla_step(
        state_gla, gla_proj_s[:, :d_gk], gla_proj_s[:, d_gk:2 * d_gk], la_s,
        gla_proj_s[:, 2 * d_gk:2 * d_gk + d], gla_proj_s[:, 2 * d_gk + d:], gla_out_norm)
    heads3 = lambda a: a.reshape(bsz, DIFF_HEADS, DIFF_DV)
    far = jnp.broadcast_to(bias_by_dist[2 * PAGE_SIZE - 1][None, :, None], (PAGE_SIZE, DIFF_HEADS, LANES))
    last = jnp.broadcast_to(bias_by_dist[PAGE_SIZE:0:-1][:, :, None], (PAGE_SIZE, DIFF_HEADS, LANES))
    bias_new = jnp.broadcast_to(bias_by_dist[0][None, :, None], (1, DIFF_HEADS, LANES))
    diff_normed_s = _diff_decode(heads3(dq_s), heads3(dk_s), heads3(dv_s), cache_diff_k, cache_diff_v, page_table,
                                 jnp.stack([far, last]), bias_new, lam, diff_subln, post_scale)
    y_sample = merge_and_rest(xs, hs, gla_gated_s, diff_normed_s.reshape(bsz, d_diff),
                              cache_mem_k.reshape(bsz, n_mem, d_mem), cache_mem_v.reshape(bsz, n_mem, d_mem))

    return (y_prompt.reshape(1, t, d), y_sample.reshape(bsz, 1, d),
            dk_p.reshape(1, t, DIFF_HEADS, DIFF_DV), dv_p.reshape(1, t, DIFF_HEADS, DIFF_DV),
            dk_s.reshape(bsz, 1, DIFF_HEADS, DIFF_DV), dv_s.reshape(bsz, 1, DIFF_HEADS, DIFF_DV),
            gla_state_prompt[None], gla_state_sample,
            mem_k_p.reshape(1, n_mem, MEM_HEADS, MEM_DH), mem_v_p.reshape(1, n_mem, MEM_HEADS, MEM_DH))
```

```python
import functools
import math

import numpy as np
import jax
import jax.numpy as jnp
from jax import lax
from jax.experimental import pallas as pl
from jax.experimental.pallas import tpu as pltpu

F32 = jnp.float32
BF16 = jnp.bfloat16

NORM_EPS = 1e-6
NEG_INF = -1e30
GLA_TAU = 16.0
GLA_HEADS = 8
GLA_RANK = 16
GLA_CHUNK = 64
GLA_SUB = 16
DIFF_HEADS = 16
DIFF_DH = 128
DIFF_DV = 256
REL_BUCKETS = 32
REL_MAX_EXACT = 16
REL_MAX_DIST = 128
MEM_HEADS = 4
MEM_DH = 128
PAGE_SIZE = 128
ATT_TILE = 512
LANES = 128
V7X_VMEM_LIMIT = 56 * 1024 * 1024

LOG2E = math.log2(math.e)

NT_DIMS = (((1,), (1,)), ((), ()))
TN_DIMS = (((0,), (0,)), ((), ()))


def _params(*semantics):
    return pltpu.CompilerParams(dimension_semantics=semantics, vmem_limit_bytes=V7X_VMEM_LIMIT)


def _silu(x):
    return x / (1.0 + jnp.exp(-x))


def _sigmoid(x):
    return 1.0 / (1.0 + jnp.exp(-x))


def _rmsnorm_kernel(x_ref, g_ref, o_ref):
    x = x_ref[...]
    ms = jnp.mean(x * x, axis=-1, keepdims=True)
    o_ref[...] = (x * lax.rsqrt(ms + NORM_EPS) * g_ref[...]).astype(o_ref.dtype)


def _rmsnorm(x, g, out_dtype):
    m, d = x.shape
    bm = min(m, 256)
    return pl.pallas_call(
        _rmsnorm_kernel,
        grid=(m // bm,),
        in_specs=[pl.BlockSpec((bm, d), lambda i: (i, 0)), pl.BlockSpec((1, d), lambda i: (0, 0))],
        out_specs=pl.BlockSpec((bm, d), lambda i: (i, 0)),
        out_shape=jax.ShapeDtypeStruct((m, d), out_dtype),
        compiler_params=_params("parallel"),
        name="rmsnorm",
    )(x, g.reshape(1, d))


def _mm_kernel(*refs, n_w, n_o, transposed, epilogue):
    x_ref, w_refs, e_refs, o_refs = refs[0], refs[1:1 + n_w], refs[1 + n_w:-n_o], refs[-n_o:]
    x = x_ref[...]
    if transposed:
        accs = [lax.dot_general(x, w[...].astype(BF16), NT_DIMS, preferred_element_type=F32) for w in w_refs]
    else:
        accs = [jnp.dot(x, w[...].astype(BF16), preferred_element_type=F32) for w in w_refs]
    outs = epilogue(accs, [e[...] for e in e_refs])
    outs = outs if isinstance(outs, tuple) else (outs,)
    for o_ref, o in zip(o_refs, outs):
        o_ref[...] = o.astype(o_ref.dtype)


def _matmul(x, weights, n_out, out_dtype, epilogue, extras=(), name="matmul", transposed=False):
    m, k = x.shape
    out_dtypes = out_dtype if isinstance(out_dtype, tuple) else (out_dtype,)
    wide = k > 4096
    bn = min(n_out, 512 if (wide and weights[0][0].dtype == BF16) else 256)
    if m <= 1024:
        bm = m
    else:
        bm = 512 if wide else 1024
    x_mode = dict(pipeline_mode=pl.Buffered(1)) if wide else {}
    in_specs = [pl.BlockSpec((bm, k), lambda i, j: (i, 0), **x_mode)]
    args = [x]
    for w, off in weights:
        if transposed:
            in_specs.append(pl.BlockSpec((bn, k), functools.partial(lambda i, j, o: (o + j, 0), o=off // bn)))
        else:
            in_specs.append(pl.BlockSpec((k, bn), functools.partial(lambda i, j, o: (0, o + j), o=off // bn)))
        args.append(w)
    for e, off in extras:
        if e.shape[0] == 1:
            in_specs.append(pl.BlockSpec((1, bn), functools.partial(lambda i, j, o: (0, o + j), o=off // bn)))
        else:
            in_specs.append(pl.BlockSpec((bm, bn), functools.partial(lambda i, j, o: (i, o + j), o=off // bn)))
        args.append(e)
    outs = pl.pallas_call(
        functools.partial(_mm_kernel, n_w=len(weights), n_o=len(out_dtypes), transposed=transposed,
                          epilogue=epilogue),
        grid=(m // bm, n_out // bn),
        in_specs=in_specs,
        out_specs=[pl.BlockSpec((bm, bn), lambda i, j: (i, j)) for _ in out_dtypes],
        out_shape=[jax.ShapeDtypeStruct((m, n_out), dt) for dt in out_dtypes],
        compiler_params=_params("parallel", "arbitrary"),
        name=name,
    )(*args)
    return outs if isinstance(out_dtype, tuple) else outs[0]


def _ep_plain(accs, extras):
    return accs[0]


def _ep_scaled(accs, extras, *, scale):
    return accs[0] * scale


def _ep_twice(accs, extras):
    return accs[0], accs[0]


def _ep_swiglu(accs, extras):
    return _silu(accs[0]) * accs[1]


def _ep_residual(accs, extras, *, scale):
    return extras[0] + scale * accs[0]


def _ep_log_decay(accs, extras):
    z = accs[0] + extras[0]
    return (jnp.minimum(z, 0.0) - jnp.log(1.0 + jnp.exp(-jnp.abs(z)))) * (1.0 / GLA_TAU)


def _ep_gate(accs, extras):
    bias_a, bias_b, a, b = extras
    return _sigmoid(accs[0] + bias_a) * a + _sigmoid(accs[1] + bias_b) * b


def _gla_prompt_kernel(q_ref, k_ref, la_ref, v_ref, r_ref, gain_ref, o_ref, st_ref, state, *, n_chunks):
    c = pl.program_id(1)
    chunk, dk = q_ref.shape

    @pl.when(c == 0)
    def _():
        state[...] = jnp.zeros_like(state)

    la = la_ref[...]
    row = lax.broadcasted_iota(jnp.int32, (chunk, chunk), 0)
    col = lax.broadcasted_iota(jnp.int32, (chunk, chunk), 1)
    tril = (col <= row).astype(F32)
    b = jnp.dot(tril, la, precision=lax.Precision.HIGHEST, preferred_element_type=F32)
    q = q_ref[...] * (dk ** -0.5)
    k = k_ref[...]
    vb = v_ref[...].astype(BF16)
    s_t = state[...]

    o = lax.dot_general((q * jnp.exp(b)).astype(BF16), s_t.astype(BF16), NT_DIMS, preferred_element_type=F32)

    key_row = lax.broadcasted_iota(jnp.int32, (chunk, 1), 0)
    lane = lax.broadcasted_iota(jnp.int32, (GLA_SUB, chunk), 1)
    sub_row = lax.broadcasted_iota(jnp.int32, (GLA_SUB, chunk), 0)
    row_blocks = []
    for i in range(chunk // GLA_SUB):
        lo = i * GLA_SUB
        b_i, q_i, k_i = b[lo:lo + GLA_SUB], q[lo:lo + GLA_SUB], k[lo:lo + GLA_SUB]
        if i == 0:
            a_rows = jnp.zeros((GLA_SUB, chunk), F32)
        else:
            ref_b = b[lo - 1:lo]
            q_t = (q_i * jnp.exp(b_i - ref_b)).astype(BF16)
            k_t = jnp.where(key_row < lo, k * jnp.exp(jnp.minimum(ref_b - b, 0.0)), 0.0).astype(BF16)
            a_rows = lax.dot_general(q_t, k_t, NT_DIMS, preferred_element_type=F32)
        diag_col = jnp.where(lane - lo <= sub_row, lane, -1)
        for s in range(GLA_SUB):
            w = jnp.exp(jnp.minimum(b_i - b_i[s:s + 1], 0.0)) * q_i * k_i[s:s + 1]
            a_rows = jnp.where(diag_col == lo + s, jnp.sum(w, axis=-1, keepdims=True), a_rows)
        row_blocks.append(a_rows)
    a = jnp.concatenate(row_blocks, axis=0)
    o = o + jnp.dot(a.astype(BF16), vb, preferred_element_type=F32)

    b_last = b[chunk - 1:chunk]
    k_dec = (k * jnp.exp(b_last - b)).astype(BF16)
    s_new = s_t * jnp.exp(b_last) + lax.dot_general(vb, k_dec, TN_DIMS, preferred_element_type=F32)
    state[...] = s_new

    @pl.when(c == n_chunks - 1)
    def _():
        st_ref[0] = s_new.T

    ms = jnp.mean(o * o, axis=-1, keepdims=True)
    o_ref[...] = (o * lax.rsqrt(ms + NORM_EPS) * gain_ref[...] * _silu(r_ref[...])).astype(o_ref.dtype)


def _gla_prompt(proj, la, gain):
    t = proj.shape[0]
    dk = la.shape[1] // GLA_HEADS
    dv = 2 * dk
    n_chunks = t // GLA_CHUNK
    h = GLA_HEADS
    return pl.pallas_call(
        functools.partial(_gla_prompt_kernel, n_chunks=n_chunks),
        grid=(h, n_chunks),
        in_specs=[
            pl.BlockSpec((GLA_CHUNK, dk), lambda i, c: (c, i)),
            pl.BlockSpec((GLA_CHUNK, dk), lambda i, c: (c, h + i)),
            pl.BlockSpec((GLA_CHUNK, dk), lambda i, c: (c, i)),
            pl.BlockSpec((GLA_CHUNK, dv), lambda i, c: (c, h + i)),
            pl.BlockSpec((GLA_CHUNK, dv), lambda i, c: (c, 2 * h + i)),
            pl.BlockSpec((1, dv), lambda i, c: (0, 0)),
        ],
        out_specs=[
            pl.BlockSpec((GLA_CHUNK, dv), lambda i, c: (c, i)),
            pl.BlockSpec((1, dk, dv), lambda i, c: (i, 0, 0)),
        ],
        out_shape=[jax.ShapeDtypeStruct((t, h * dv), BF16), jax.ShapeDtypeStruct((h, dk, dv), F32)],
        scratch_shapes=[pltpu.VMEM((dv, dk), F32)],
        compiler_params=_params("parallel", "arbitrary"),
        name="gla_prompt",
    )(proj, proj, la, proj, proj, gain.reshape(1, dv))


def _gla_step_kernel(s_ref, q_ref, k_ref, la_ref, v_ref, r_ref, gain_ref, o_ref, sn_ref):
    heads, dk = s_ref.shape[1], s_ref.shape[2]
    for h in range(heads):
        s_new = jnp.exp(la_ref[0, h]) * s_ref[0, h] + k_ref[0, h] * v_ref[0, h]
        sn_ref[0, h] = s_new
        o = jnp.sum((q_ref[0, h] * (dk ** -0.5)) * s_new, axis=0, keepdims=True)
        ms = jnp.mean(o * o, axis=-1, keepdims=True)
        o_ref[0, h] = (o * lax.rsqrt(ms + NORM_EPS) * gain_ref[...] * _silu(r_ref[0, h])).astype(o_ref.dtype)


def _gla_step(state, q, k, la, v, r, gain):
    bsz, h, dk, dv = state.shape
    col = lambda a: a.reshape(bsz, h, dk, 1)
    row = lambda a: a.reshape(bsz, h, 1, dv)
    col_spec = pl.BlockSpec((1, h, dk, 1), lambda i: (i, 0, 0, 0))
    row_spec = pl.BlockSpec((1, h, 1, dv), lambda i: (i, 0, 0, 0))
    state_spec = pl.BlockSpec((1, h, dk, dv), lambda i: (i, 0, 0, 0))
    out, new_state = pl.pallas_call(
        _gla_step_kernel,
        grid=(bsz,),
        in_specs=[state_spec, col_spec, col_spec, col_spec, row_spec, row_spec, pl.BlockSpec((1, dv), lambda i: (0, 0))],
        out_specs=[row_spec, state_spec],
        out_shape=[jax.ShapeDtypeStruct((bsz, h, 1, dv), BF16), jax.ShapeDtypeStruct(state.shape, F32)],
        compiler_params=_params("parallel"),
        name="gla_step",
    )(state, col(q), col(k), col(la), row(v), row(r), gain.reshape(1, dv))
    return out.reshape(bsz, h * dv), new_state


def _rel_bucket(n):
    n = np.asarray(n, np.int64)
    nf = np.maximum(n, 1).astype(np.float64)
    large = REL_MAX_EXACT + (np.log(nf / REL_MAX_EXACT) / math.log(REL_MAX_DIST / REL_MAX_EXACT)
                             * (REL_BUCKETS - REL_MAX_EXACT)).astype(np.int64)
    return np.where(n < REL_MAX_EXACT, n, np.minimum(large, REL_BUCKETS - 1)).astype(np.int32)


def _bias_tiles_kernel(table_ref, idx_ref, o_ref):
    h = pl.program_id(0)
    for v in range(idx_ref.shape[0]):
        idx = idx_ref[v]
        acc = jnp.where(idx < 0, NEG_INF, 0.0).astype(F32)
        for bkt in range(REL_BUCKETS):
            acc = jnp.where(idx == bkt, table_ref[bkt, h] * LOG2E, acc)
        o_ref[0, v] = acc


def _bias_tiles(table, tile):
    assert tile >= REL_MAX_DIST
    key = np.arange(tile)[:, None]
    qry = np.arange(tile)[None, :]
    diag = np.where(key <= qry, _rel_bucket(np.maximum(qry - key, 0)), -1)
    idx = np.stack([diag, _rel_bucket(tile + qry - key)]).astype(np.int32)
    heads = table.shape[1]
    return pl.pallas_call(
        _bias_tiles_kernel,
        grid=(heads,),
        in_specs=[pl.BlockSpec(memory_space=pltpu.SMEM), pl.BlockSpec((2, tile, tile), lambda h: (0, 0, 0))],
        out_specs=pl.BlockSpec((1, 2, tile, tile), lambda h: (h, 0, 0, 0)),
        out_shape=jax.ShapeDtypeStruct((heads, 2, tile, tile), F32),
        compiler_params=_params("parallel"),
        name="bias_tiles",
    )(table, jnp.asarray(idx))


def _scalars_kernel(table_ref, onehot_ref, q1_ref, k1_ref, q2_ref, k2_ref, bias_ref, lam_ref, *, lam_init):
    bias_ref[...] = jnp.dot(onehot_ref[...], table_ref[...], precision=lax.Precision.HIGHEST,
                            preferred_element_type=F32) * LOG2E
    s1 = jnp.sum(q1_ref[...] * k1_ref[...], axis=-1, keepdims=True)
    s2 = jnp.sum(q2_ref[...] * k2_ref[...], axis=-1, keepdims=True)
    lam_ref[...] = jnp.broadcast_to(jnp.exp(s1) - jnp.exp(s2) + lam_init, lam_ref.shape)


def _scalars(table, lq1, lk1, lq2, lk2, lam_init, n_dist):
    onehot = np.zeros((n_dist, REL_BUCKETS), np.float32)
    onehot[np.arange(n_dist), _rel_bucket(np.arange(n_dist))] = 1.0
    vec = lambda a: a.reshape(1, -1)
    return pl.pallas_call(
        functools.partial(_scalars_kernel, lam_init=lam_init),
        out_shape=[jax.ShapeDtypeStruct((n_dist, table.shape[1]), F32), jax.ShapeDtypeStruct((1, LANES), F32)],
        name="bias_lambda",
    )(table, jnp.asarray(onehot), vec(lq1), vec(lk1), vec(lq2), vec(lk2))


def _diff_finalize(acc1, l1, acc2, l2, lam, gain, post_scale):
    o = acc1 / l1 - lam * (acc2 / l2)
    ms = jnp.mean(o * o, axis=-1, keepdims=True)
    return o * lax.rsqrt(ms + NORM_EPS) * gain * post_scale


def _flash_kernel(qi_ref, kj_ref, far_ref, q_ref, k_ref, vt_ref, bias_ref, lam_ref, gain_ref, o_ref,
                  m_ref, l_ref, acc_ref, *, post_scale):
    h = pl.program_id(0)
    step = pl.program_id(1)
    qi, kj = qi_ref[step], kj_ref[step]
    tq = q_ref.shape[0]

    @pl.when(kj == 0)
    def _():
        m_ref[...] = jnp.full_like(m_ref, NEG_INF)
        l_ref[...] = jnp.zeros_like(l_ref)
        acc_ref[...] = jnp.zeros_like(acc_ref)

    def update(near, shift):
        scores = []
        for c in range(2):
            s_t = lax.dot_general(k_ref[:, c * DIFF_DH:(c + 1) * DIFF_DH], q_ref[:, c * DIFF_DH:(c + 1) * DIFF_DH],
                                  NT_DIMS, preferred_element_type=F32)
            scores.append(s_t + bias_ref[0, qi - kj] if near else s_t)
        probs, alphas = [], []
        for c in range(2):
            m_prev = m_ref[c]
            m_new = jnp.maximum(m_prev, jnp.max(scores[c], axis=0, keepdims=True) + shift)
            alpha = jnp.exp2(m_prev - m_new)
            p_t = jnp.exp2(scores[c] - (m_new[:1] - shift))
            l_ref[c] = alpha * l_ref[c] + jnp.sum(p_t, axis=0, keepdims=True)
            m_ref[c] = m_new
            probs.append(p_t.astype(BF16))
            alphas.append(alpha[:1])
        for c in range(2):
            acc_ref[c] = acc_ref[c] * alphas[c] + jnp.dot(vt_ref[...], probs[c], preferred_element_type=F32)

    @pl.when(qi - kj >= 2)
    def _():
        update(False, far_ref[h])

    @pl.when(qi - kj < 2)
    def _():
        update(True, 0.0)

    @pl.when(kj == qi)
    def _():
        lam = jnp.concatenate([lam_ref[...]] * (tq // LANES), axis=1)
        o_t = acc_ref[0] / l_ref[0][:1] - lam * (acc_ref[1] / l_ref[1][:1])
        ms = jnp.mean(o_t * o_t, axis=0, keepdims=True)
        y_t = o_t * lax.rsqrt(ms + NORM_EPS) * gain_ref[...] * post_scale
        o_ref[...] = y_t.T.astype(o_ref.dtype)


def _diff_prompt(q, k, v_t, bias_tiles, far_bias, lam, gain, post_scale):
    t = q.shape[0]
    tile = bias_tiles.shape[-1]
    n_tiles = t // tile
    pairs = [(i, j) for i in range(n_tiles) for j in range(i + 1)]
    qi_tab = jnp.asarray(np.array([p[0] for p in pairs], np.int32))
    kj_tab = jnp.asarray(np.array([p[1] for p in pairs], np.int32))
    grid_spec = pltpu.PrefetchScalarGridSpec(
        num_scalar_prefetch=3,
        grid=(DIFF_HEADS, len(pairs)),
        in_specs=[
            pl.BlockSpec((tile, DIFF_DV), lambda h, s, qi, kj, fb: (qi[s], h)),
            pl.BlockSpec((tile, DIFF_DV), lambda h, s, qi, kj, fb: (kj[s], h)),
            pl.BlockSpec((DIFF_DV, tile), lambda h, s, qi, kj, fb: (h, kj[s])),
            pl.BlockSpec((1, 2, tile, tile), lambda h, s, qi, kj, fb: (h, 0, 0, 0)),
            pl.BlockSpec((1, LANES), lambda h, s, qi, kj, fb: (0, 0)),
            pl.BlockSpec((DIFF_DV, tile), lambda h, s, qi, kj, fb: (0, 0)),
        ],
        out_specs=pl.BlockSpec((tile, DIFF_DV), lambda h, s, qi, kj, fb: (qi[s], h)),
        scratch_shapes=[pltpu.VMEM((2, 8, tile), F32), pltpu.VMEM((2, 8, tile), F32),
                        pltpu.VMEM((2, DIFF_DV, tile), F32)],
    )
    gain_cols = jnp.broadcast_to(gain.reshape(DIFF_DV, 1), (DIFF_DV, tile))
    return pl.pallas_call(
        functools.partial(_flash_kernel, post_scale=post_scale),
        grid_spec=grid_spec,
        out_shape=jax.ShapeDtypeStruct((t, DIFF_HEADS * DIFF_DV), BF16),
        compiler_params=_params("parallel", "arbitrary"),
        name="diff_attn_prompt",
    )(qi_tab, kj_tab, far_bias, q, k, v_t, bias_tiles, lam, gain_cols)


def _decode_kernel(pt_ref, q_ref, kn_ref, vn_ref, ka_ref, va_ref, kb_ref, vb_ref, bias_far_ref, bias_last_ref,
                   bias_new_ref, lam_ref, gain_ref, o_ref, m_ref, l_ref, acc_ref, *, n_steps, post_scale):
    step = pl.program_id(1)

    @pl.when(step == 0)
    def _():
        m_ref[...] = jnp.full_like(m_ref, NEG_INF)
        l_ref[...] = jnp.zeros_like(l_ref)
        acc_ref[...] = jnp.zeros_like(acc_ref)

    q = q_ref[0] * (LOG2E * DIFF_DH ** -0.5)

    def absorb(keys, values, bias, shift):
        prod = keys * q[None]
        for c in range(2):
            s = jnp.sum(prod[:, :, c * DIFF_DH:(c + 1) * DIFF_DH], axis=-1, keepdims=True)
            if bias is not None:
                s = s + bias
            m_prev = m_ref[c]
            m_new = jnp.maximum(m_prev, jnp.max(s, axis=0) + shift)
            alpha = jnp.exp2(m_prev - m_new)
            p = jnp.exp2(s - (m_new - shift)[None])
            l_ref[c] = alpha * l_ref[c] + jnp.sum(p, axis=0)
            m_ref[c] = m_new
            weights = jnp.concatenate([p] * (DIFF_DV // LANES), axis=-1)
            acc_ref[c] = (acc_ref[c] * jnp.concatenate([alpha] * (DIFF_DV // LANES), axis=-1)
                          + jnp.sum(weights * values, axis=0))

    absorb(ka_ref[0], va_ref[0], None, bias_far_ref[...])

    @pl.when(step < n_steps - 1)
    def _():
        absorb(kb_ref[0], vb_ref[0], None, bias_far_ref[...])

    @pl.when(step == n_steps - 1)
    def _():
        absorb(kb_ref[0], vb_ref[0], bias_last_ref[...], 0.0)
        absorb(kn_ref[...], vn_ref[...], bias_new_ref[...], 0.0)
        rep = lambda a: jnp.concatenate([a] * (DIFF_DV // LANES), axis=-1)
        o_ref[0] = _diff_finalize(acc_ref[0], rep(l_ref[0]), acc_ref[1], rep(l_ref[1]), rep(lam_ref[...]),
                                  gain_ref[...], post_scale).astype(o_ref.dtype)


def _diff_decode(q, k_new, v_new, cache_k, cache_v, page_table, bias_far, bias_last, bias_new, lam, gain,
                 post_scale):
    bsz, n_pages = page_table.shape
    assert n_pages % 2 == 0
    n_steps = n_pages // 2
    blk = (1, DIFF_HEADS, DIFF_DV)
    tok_spec = pl.BlockSpec(blk, lambda b, p, pt: (b, 0, 0))
    page_a = pl.BlockSpec((1, PAGE_SIZE, DIFF_HEADS, DIFF_DV), lambda b, p, pt: (pt[b, 2 * p], 0, 0, 0))
    page_b = pl.BlockSpec((1, PAGE_SIZE, DIFF_HEADS, DIFF_DV), lambda b, p, pt: (pt[b, 2 * p + 1], 0, 0, 0))
    grid_spec = pltpu.PrefetchScalarGridSpec(
        num_scalar_prefetch=1,
        grid=(bsz, n_steps),
        in_specs=[
            tok_spec, tok_spec, tok_spec, page_a, page_a, page_b, page_b,
            pl.BlockSpec((DIFF_HEADS, LANES), lambda b, p, pt: (0, 0)),
            pl.BlockSpec((PAGE_SIZE, DIFF_HEADS, LANES), lambda b, p, pt: (0, 0, 0)),
            pl.BlockSpec((1, DIFF_HEADS, LANES), lambda b, p, pt: (0, 0, 0)),
            pl.BlockSpec((1, LANES), lambda b, p, pt: (0, 0)),
            pl.BlockSpec((1, DIFF_DV), lambda b, p, pt: (0, 0)),
        ],
        out_specs=tok_spec,
        scratch_shapes=[pltpu.VMEM((2, DIFF_HEADS, LANES), F32), pltpu.VMEM((2, DIFF_HEADS, LANES), F32),
                        pltpu.VMEM((2, DIFF_HEADS, DIFF_DV), F32)],
    )
    return pl.pallas_call(
        functools.partial(_decode_kernel, n_steps=n_steps, post_scale=post_scale),
        grid_spec=grid_spec,
        out_shape=jax.ShapeDtypeStruct((bsz, DIFF_HEADS, DIFF_DV), BF16),
        compiler_params=_params("parallel", "arbitrary"),
        name="diff_attn_decode",
    )(page_table, q, k_new, v_new, cache_k, cache_v, cache_k, cache_v, bias_far, bias_last, bias_new, lam,
      gain.reshape(1, DIFF_DV))


def _mem_attn_kernel(q_ref, k_ref, v_ref, o_ref):
    outs = []
    for h in range(MEM_HEADS):
        sl = slice(h * MEM_DH, (h + 1) * MEM_DH)
        s = lax.dot_general(q_ref[0, :, sl], k_ref[0, :, sl].astype(BF16), NT_DIMS,
                            preferred_element_type=F32) * (MEM_DH ** -0.5)
        p = jnp.exp(s - jnp.max(s, axis=-1, keepdims=True))
        p = p / jnp.sum(p, axis=-1, keepdims=True)
        outs.append(jnp.dot(p.astype(BF16), v_ref[0, :, sl].astype(BF16), preferred_element_type=F32))
    o_ref[0] = jnp.concatenate(outs, axis=-1).astype(o_ref.dtype)


def _mem_attn(q, mem_k, mem_v):
    bsz, t, d = q.shape
    n_mem = mem_k.shape[1]
    bt = min(t, 512)
    return pl.pallas_call(
        _mem_attn_kernel,
        grid=(bsz, t // bt),
        in_specs=[pl.BlockSpec((1, bt, d), lambda b, i: (b, i, 0)),
                  pl.BlockSpec((1, n_mem, d), lambda b, i: (b, 0, 0)),
                  pl.BlockSpec((1, n_mem, d), lambda b, i: (b, 0, 0))],
        out_specs=pl.BlockSpec((1, bt, d), lambda b, i: (b, i, 0)),
        out_shape=jax.ShapeDtypeStruct((bsz, t, d), BF16),
        compiler_params=_params("parallel", "parallel"),
        name="mem_attn",
    )(q, mem_k, mem_v)


def _swiglu_half_step(x, norm_g, w1, w3, w2):
    h = _rmsnorm(x, norm_g, BF16)
    d_ff = w1.shape[1]
    gu = _matmul(h, [(w1, 0), (w3, 0)], d_ff, BF16, _ep_swiglu, name="ffn_up")
    return _matmul(gu, [(w2.astype(BF16), 0)], x.shape[1], F32, functools.partial(_ep_residual, scale=0.5),
                   [(x, 0)], name="ffn_down")


def kernel(x_prompt, x_sample, cache_diff_k, cache_diff_v, cache_mem_k, cache_mem_v, state_gla, page_table, mem_prompt, rel_bias_table, ffn_a_norm, ffn_a_w1, ffn_a_w3, ffn_a_w2, mix_norm, w_in, gla_w_alpha, gla_b_alpha, gla_out_norm, diff_lambda_q1, diff_lambda_k1, diff_lambda_q2, diff_lambda_k2, diff_subln, w_branch_gla, w_branch_diff, w_gate, b_gate, w_out, cross_norm, mem_norm, w_mq, w_mk, w_mv, w_mo, ffn_b_norm, ffn_b_w1, ffn_b_w3, ffn_b_w2, final_norm):
    t, d = x_prompt.shape[1], x_prompt.shape[2]
    bsz = x_sample.shape[0]
    n_mem = mem_prompt.shape[1]
    d_mem = MEM_HEADS * MEM_DH
    d_gk = gla_w_alpha.shape[1]
    d_gla = 2 * d_gk + 2 * d
    d_diff = DIFF_HEADS * DIFF_DV
    lam_init = 0.8 - 0.6 * math.exp(-0.3 * 0)
    post_scale = 1.0 - lam_init

    xp = x_prompt.reshape(t, d)
    xs = x_sample.reshape(bsz, d)

    w_in_t = w_in.T
    w_alpha_in_t = jnp.pad(w_in_t[d_gla:d_gla + GLA_RANK], ((0, LANES - GLA_RANK), (0, 0)))
    w_diff_t = w_in_t[d_gla + GLA_RANK:]
    w_alpha = jnp.pad(gla_w_alpha, ((0, LANES - GLA_RANK), (0, 0)))
    b_alpha = gla_b_alpha.reshape(1, d_gk)
    b_gate2 = b_gate.reshape(1, 2 * d)

    bias_by_dist, lam = _scalars(rel_bias_table, diff_lambda_q1, diff_lambda_k1, diff_lambda_q2, diff_lambda_k2,
                                 lam_init, 2 * PAGE_SIZE)
    bias_tiles = _bias_tiles(rel_bias_table, ATT_TILE)
    far_bias = bias_by_dist[2 * PAGE_SIZE - 1]

    def mix_inputs(x, q_dtype, q_scale):
        h = _rmsnorm(x, mix_norm, BF16)
        gla_proj = _matmul(h, [(w_in_t, 0)], d_gla, F32, _ep_plain, name="proj_gla", transposed=True)
        a_low = _matmul(h, [(w_alpha_in_t, 0)], LANES, BF16, _ep_plain, name="proj_alpha", transposed=True)
        la = _matmul(a_low, [(w_alpha, 0)], d_gk, F32, _ep_log_decay, [(b_alpha, 0)], name="log_decay")
        dq = _matmul(h, [(w_diff_t, 0)], d_diff, q_dtype, functools.partial(_ep_scaled, scale=q_scale),
                     name="proj_dq", transposed=True)
        dk, dk_b = _matmul(h, [(w_diff_t, d_diff)], d_diff, (F32, BF16), _ep_twice, name="proj_dk",
                           transposed=True)
        dv, dv_b = _matmul(h, [(w_diff_t, 2 * d_diff)], d_diff, (F32, BF16), _ep_twice, name="proj_dv",
                           transposed=True)
        return h, gla_proj, la, dq, dk, dv, dk_b, dv_b

    def merge_and_rest(x, h, gla_gated, diff_normed, mem_k, mem_v):
        gla_out = _matmul(gla_gated, [(w_branch_gla, 0)], d, F32, _ep_plain, name="branch_gla")
        diff_out = _matmul(diff_normed, [(w_branch_diff, 0)], d, F32, _ep_plain, name="branch_diff")
        merged = _matmul(h, [(w_gate, 0), (w_gate, d)], d, BF16, _ep_gate,
                         [(b_gate2, 0), (b_gate2, d), (gla_out, 0), (diff_out, 0)], name="gate_merge")
        x = _matmul(merged, [(w_out, 0)], d, F32, functools.partial(_ep_residual, scale=1.0), [(x, 0)],
                    name="merge_out")
        hc = _rmsnorm(x, cross_norm, BF16)
        qm = _matmul(hc, [(w_mq, 0)], d_mem, BF16, _ep_plain, name="mem_q")
        rows = qm.shape[0] // mem_k.shape[0]
        qm = qm.reshape(mem_k.shape[0], rows, d_mem)
        pad = (-rows) % 8
        om = _mem_attn(jnp.pad(qm, ((0, 0), (0, pad), (0, 0))), mem_k, mem_v)[:, :rows].reshape(-1, d_mem)
        x = _matmul(om, [(w_mo, 0)], d, F32, functools.partial(_ep_residual, scale=1.0), [(x, 0)], name="mem_out")
        x = _swiglu_half_step(x, ffn_b_norm, ffn_b_w1, ffn_b_w3, ffn_b_w2)
        return _rmsnorm(x, final_norm, F32)

    xp = _swiglu_half_step(xp, ffn_a_norm, ffn_a_w1, ffn_a_w3, ffn_a_w2)
    hp, gla_proj_p, la_p, dq_p, dk_p, dv_p, dk_pb, dv_pb = mix_inputs(xp, BF16, LOG2E * DIFF_DH ** -0.5)
    gla_gated_p, gla_state_prompt = _gla_prompt(gla_proj_p, la_p, gla_out_norm)
    diff_normed_p = _diff_prompt(dq_p, dk_pb, dv_pb.T, bias_tiles, far_bias, lam, diff_subln, post_scale)
    mem_normed = _rmsnorm(mem_prompt.reshape(n_mem, d), mem_norm, BF16)
    mem_k_p = _matmul(mem_normed, [(w_mk, 0)], d_mem, F32, _ep_plain, name="mem_k")
    mem_v_p = _matmul(mem_normed, [(w_mv, 0)], d_mem, F32, _ep_plain, name="mem_v")
    y_prompt = merge_and_rest(xp, hp, gla_gated_p, diff_normed_p, mem_k_p[None], mem_v_p[None])

    xs = _swiglu_half_step(xs, ffn_a_norm, ffn_a_w1, ffn_a_w3, ffn_a_w2)
    hs, gla_proj_s, la_s, dq_s, dk_s, dv_s, _, _ = mix_inputs(xs, F32, 1.0)
    gla_gated_s, gla_state_sample = _gla_step(
        state_gla, gla_proj_s[:, :d_gk], gla_proj_s[:, d_gk:2 * d_gk], la_s,
        gla_proj_s[:, 2 * d_gk:2 * d_gk + d], gla_proj_s[:, 2 * d_gk + d:], gla_out_norm)
    heads3 = lambda a: a.reshape(bsz, DIFF_HEADS, DIFF_DV)
    bias_far = jnp.broadcast_to(far_bias[:, None], (DIFF_HEADS, LANES))
    bias_last = jnp.broadcast_to(bias_by_dist[PAGE_SIZE:0:-1][:, :, None], (PAGE_SIZE, DIFF_HEADS, LANES))
    bias_new = jnp.broadcast_to(bias_by_dist[0][None, :, None], (1, DIFF_HEADS, LANES))
    diff_normed_s = _diff_decode(heads3(dq_s), heads3(dk_s), heads3(dv_s), cache_diff_k, cache_diff_v, page_table,
                                 bias_far, bias_last, bias_new, lam, diff_subln, post_scale)
    y_sample = merge_and_rest(xs, hs, gla_gated_s, diff_normed_s.reshape(bsz, d_diff),
                              cache_mem_k.reshape(bsz, n_mem, d_mem), cache_mem_v.reshape(bsz, n_mem, d_mem))

    return (y_prompt.reshape(1, t, d), y_sample.reshape(bsz, 1, d),
            dk_p.reshape(1, t, DIFF_HEADS, DIFF_DV), dv_p.reshape(1, t, DIFF_HEADS, DIFF_DV),
            dk_s.reshape(bsz, 1, DIFF_HEADS, DIFF_DV), dv_s.reshape(bsz, 1, DIFF_HEADS, DIFF_DV),
            gla_state_prompt[None], gla_state_sample,
            mem_k_p.reshape(1, n_mem, MEM_HEADS, MEM_DH), mem_v_p.reshape(1, n_mem, MEM_HEADS, MEM_DH))
```

```python
import functools
import math

import numpy as np
import jax
import jax.numpy as jnp
from jax import lax
from jax.experimental import pallas as pl
from jax.experimental.pallas import tpu as pltpu

F32 = jnp.float32
BF16 = jnp.bfloat16

NORM_EPS = 1e-6
NEG_INF = -1e30
GLA_TAU = 16.0
GLA_HEADS = 8
GLA_RANK = 16
GLA_CHUNK = 64
GLA_SUB = 16
GLA_HEADS_PER_STEP = 4
DIFF_HEADS = 16
DIFF_DH = 128
DIFF_DV = 256
REL_BUCKETS = 32
REL_MAX_EXACT = 16
REL_MAX_DIST = 128
MEM_HEADS = 4
MEM_DH = 128
PAGE_SIZE = 128
DECODE_CHUNK = 16
ATT_TQ = 1024
ATT_TK = 512
ATT_ONES_ROWS = 16
LANES = 128
V7X_VMEM_LIMIT = 56 * 1024 * 1024

LOG2E = math.log2(math.e)

NT_DIMS = (((1,), (1,)), ((), ()))
TN_DIMS = (((0,), (0,)), ((), ()))


def _params(*semantics):
    return pltpu.CompilerParams(dimension_semantics=semantics, vmem_limit_bytes=V7X_VMEM_LIMIT)


def _silu(x):
    return x / (1.0 + jnp.exp(-x))


def _sigmoid(x):
    return 1.0 / (1.0 + jnp.exp(-x))


def _rmsnorm_kernel(x_ref, g_ref, o_ref):
    x = x_ref[...]
    ms = jnp.mean(x * x, axis=-1, keepdims=True)
    o_ref[...] = (x * lax.rsqrt(ms + NORM_EPS) * g_ref[...]).astype(o_ref.dtype)


def _rmsnorm(x, g, out_dtype):
    m, d = x.shape
    bm = min(m, 256)
    return pl.pallas_call(
        _rmsnorm_kernel,
        grid=(m // bm,),
        in_specs=[pl.BlockSpec((bm, d), lambda i: (i, 0)), pl.BlockSpec((1, d), lambda i: (0, 0))],
        out_specs=pl.BlockSpec((bm, d), lambda i: (i, 0)),
        out_shape=jax.ShapeDtypeStruct((m, d), out_dtype),
        compiler_params=_params("parallel"),
        name="rmsnorm",
    )(x, g.reshape(1, d))


def _mm_kernel(*refs, n_w, n_o, transposed, epilogue):
    x_ref, w_refs, e_refs, o_refs = refs[0], refs[1:1 + n_w], refs[1 + n_w:-n_o], refs[-n_o:]
    x = x_ref[...]
    if transposed:
        accs = [lax.dot_general(x, w[...].astype(BF16), NT_DIMS, preferred_element_type=F32) for w in w_refs]
    else:
        accs = [jnp.dot(x, w[...].astype(BF16), preferred_element_type=F32) for w in w_refs]
    outs = epilogue(accs, [e[...] for e in e_refs])
    outs = outs if isinstance(outs, tuple) else (outs,)
    for o_ref, o in zip(o_refs, outs):
        o_ref[...] = o.astype(o_ref.dtype)


def _matmul(x, weights, n_out, out_dtype, epilogue, extras=(), name="matmul", transposed=False):
    m, k = x.shape
    out_dtypes = out_dtype if isinstance(out_dtype, tuple) else (out_dtype,)
    wide = k > 4096
    bn = min(n_out, 512 if (wide and weights[0][0].dtype == BF16) else 256)
    if m <= 1024:
        bm = m
    else:
        bm = 512 if wide else 1024
    x_mode = dict(pipeline_mode=pl.Buffered(1)) if wide else {}
    in_specs = [pl.BlockSpec((bm, k), lambda i, j: (i, 0), **x_mode)]
    args = [x]
    for w, off in weights:
        if transposed:
            in_specs.append(pl.BlockSpec((bn, k), functools.partial(lambda i, j, o: (o + j, 0), o=off // bn)))
        else:
            in_specs.append(pl.BlockSpec((k, bn), functools.partial(lambda i, j, o: (0, o + j), o=off // bn)))
        args.append(w)
    for e, off in extras:
        if e.shape[0] == 1:
            in_specs.append(pl.BlockSpec((1, bn), functools.partial(lambda i, j, o: (0, o + j), o=off // bn)))
        else:
            in_specs.append(pl.BlockSpec((bm, bn), functools.partial(lambda i, j, o: (i, o + j), o=off // bn)))
        args.append(e)
    outs = pl.pallas_call(
        functools.partial(_mm_kernel, n_w=len(weights), n_o=len(out_dtypes), transposed=transposed,
                          epilogue=epilogue),
        grid=(m // bm, n_out // bn),
        in_specs=in_specs,
        out_specs=[pl.BlockSpec((bm, bn), lambda i, j: (i, j)) for _ in out_dtypes],
        out_shape=[jax.ShapeDtypeStruct((m, n_out), dt) for dt in out_dtypes],
        compiler_params=_params("parallel", "arbitrary"),
        name=name,
    )(*args)
    return outs if isinstance(out_dtype, tuple) else outs[0]


def _ep_plain(accs, extras):
    return accs[0]


def _ep_scaled(accs, extras, *, scale):
    return accs[0] * scale


def _ep_twice(accs, extras):
    return accs[0], accs[0]


def _ep_swiglu(accs, extras):
    return _silu(accs[0]) * accs[1]


def _ep_residual(accs, extras, *, scale):
    return extras[0] + scale * accs[0]


def _ep_log_decay(accs, extras):
    z = accs[0] + extras[0]
    return (jnp.minimum(z, 0.0) - jnp.log(1.0 + jnp.exp(-jnp.abs(z)))) * (1.0 / GLA_TAU)


def _ep_gate(accs, extras):
    bias_a, bias_b, a, b = extras
    return _sigmoid(accs[0] + bias_a) * a + _sigmoid(accs[1] + bias_b) * b


def _gla_prompt_kernel(q_ref, k_ref, la_ref, v_ref, r_ref, gain_ref, o_ref, st_ref, state, *, n_chunks):
    c = pl.program_id(1)
    heads, dv, dk = state.shape

    @pl.when(c == 0)
    def _():
        state[...] = jnp.zeros_like(state)

    chunk = q_ref.shape[0]
    n_sub = chunk // GLA_SUB
    hs = range(heads)
    ks = [slice(hh * dk, (hh + 1) * dk) for hh in hs]
    vs = [slice(hh * dv, (hh + 1) * dv) for hh in hs]
    row = lax.broadcasted_iota(jnp.int32, (chunk, chunk), 0)
    col = lax.broadcasted_iota(jnp.int32, (chunk, chunk), 1)
    tril = (col <= row).astype(F32)
    key_row = lax.broadcasted_iota(jnp.int32, (chunk, 1), 0)
    lane = lax.broadcasted_iota(jnp.int32, (GLA_SUB, chunk), 1)
    sub_row = lax.broadcasted_iota(jnp.int32, (GLA_SUB, chunk), 0)

    b = [jnp.dot(tril, la_ref[:, ks[hh]], precision=lax.Precision.HIGHEST, preferred_element_type=F32) for hh in hs]
    q = [q_ref[:, ks[hh]] * (dk ** -0.5) for hh in hs]
    k = [k_ref[:, ks[hh]] for hh in hs]
    vb = [v_ref[:, vs[hh]].astype(BF16) for hh in hs]
    s_t = [state[hh] for hh in hs]

    o = [lax.dot_general((q[hh] * jnp.exp(b[hh])).astype(BF16), s_t[hh].astype(BF16), NT_DIMS,
                         preferred_element_type=F32) for hh in hs]

    a_rows = [[None] * n_sub for _ in hs]
    for hh in hs:
        for i in range(n_sub):
            lo = i * GLA_SUB
            if i == 0:
                a_rows[hh][i] = jnp.zeros((GLA_SUB, chunk), F32)
            else:
                ref_b = b[hh][lo - 1:lo]
                q_t = (q[hh][lo:lo + GLA_SUB] * jnp.exp(b[hh][lo:lo + GLA_SUB] - ref_b)).astype(BF16)
                k_t = jnp.where(key_row < lo, k[hh] * jnp.exp(jnp.minimum(ref_b - b[hh], 0.0)), 0.0).astype(BF16)
                a_rows[hh][i] = lax.dot_general(q_t, k_t, NT_DIMS, preferred_element_type=F32)
    for hh in hs:
        for i in range(n_sub):
            lo = i * GLA_SUB
            b_i, q_i, k_i = b[hh][lo:lo + GLA_SUB], q[hh][lo:lo + GLA_SUB], k[hh][lo:lo + GLA_SUB]
            diag_col = jnp.where(lane - lo <= sub_row, lane, -1)
            rows = a_rows[hh][i]
            for s in range(GLA_SUB):
                w = jnp.exp(jnp.minimum(b_i - b_i[s:s + 1], 0.0)) * q_i * k_i[s:s + 1]
                rows = jnp.where(diag_col == lo + s, jnp.sum(w, axis=-1, keepdims=True), rows)
            a_rows[hh][i] = rows

    for hh in hs:
        a = jnp.concatenate(a_rows[hh], axis=0)
        o_h = o[hh] + jnp.dot(a.astype(BF16), vb[hh], preferred_element_type=F32)
        b_last = b[hh][chunk - 1:chunk]
        k_dec = (k[hh] * jnp.exp(b_last - b[hh])).astype(BF16)
        state[hh] = s_t[hh] * jnp.exp(b_last) + lax.dot_general(vb[hh], k_dec, TN_DIMS,
                                                               preferred_element_type=F32)
        ms = jnp.mean(o_h * o_h, axis=-1, keepdims=True)
        o_ref[:, vs[hh]] = (o_h * lax.rsqrt(ms + NORM_EPS) * gain_ref[...]
                            * _silu(r_ref[:, vs[hh]])).astype(o_ref.dtype)

    @pl.when(c == n_chunks - 1)
    def _():
        for hh in hs:
            st_ref[hh] = state[hh].T


def _gla_prompt(proj, la, gain):
    t = proj.shape[0]
    dk = la.shape[1] // GLA_HEADS
    dv = 2 * dk
    n_chunks = t // GLA_CHUNK
    hb = GLA_HEADS_PER_STEP
    groups = GLA_HEADS // hb
    return pl.pallas_call(
        functools.partial(_gla_prompt_kernel, n_chunks=n_chunks),
        grid=(groups, n_chunks),
        in_specs=[
            pl.BlockSpec((GLA_CHUNK, hb * dk), lambda i, c: (c, i)),
            pl.BlockSpec((GLA_CHUNK, hb * dk), lambda i, c: (c, groups + i)),
            pl.BlockSpec((GLA_CHUNK, hb * dk), lambda i, c: (c, i)),
            pl.BlockSpec((GLA_CHUNK, hb * dv), lambda i, c: (c, groups + i)),
            pl.BlockSpec((GLA_CHUNK, hb * dv), lambda i, c: (c, 2 * groups + i)),
            pl.BlockSpec((1, dv), lambda i, c: (0, 0)),
        ],
        out_specs=[
            pl.BlockSpec((GLA_CHUNK, hb * dv), lambda i, c: (c, i)),
            pl.BlockSpec((hb, dk, dv), lambda i, c: (i, 0, 0)),
        ],
        out_shape=[jax.ShapeDtypeStruct((t, GLA_HEADS * dv), BF16), jax.ShapeDtypeStruct((GLA_HEADS, dk, dv), F32)],
        scratch_shapes=[pltpu.VMEM((hb, dv, dk), F32)],
        compiler_params=_params("parallel", "arbitrary"),
        name="gla_prompt",
    )(proj, proj, la, proj, proj, gain.reshape(1, dv))


def _gla_step_kernel(s_ref, q_ref, k_ref, la_ref, v_ref, r_ref, gain_ref, o_ref, sn_ref):
    heads, dk = s_ref.shape[1], s_ref.shape[2]
    for h in range(heads):
        s_new = jnp.exp(la_ref[0, h]) * s_ref[0, h] + k_ref[0, h] * v_ref[0, h]
        sn_ref[0, h] = s_new
        o = jnp.sum((q_ref[0, h] * (dk ** -0.5)) * s_new, axis=0, keepdims=True)
        ms = jnp.mean(o * o, axis=-1, keepdims=True)
        o_ref[0, h] = (o * lax.rsqrt(ms + NORM_EPS) * gain_ref[...] * _silu(r_ref[0, h])).astype(o_ref.dtype)


def _gla_step(state, q, k, la, v, r, gain):
    bsz, h, dk, dv = state.shape
    col = lambda a: a.reshape(bsz, h, dk, 1)
    row = lambda a: a.reshape(bsz, h, 1, dv)
    col_spec = pl.BlockSpec((1, h, dk, 1), lambda i: (i, 0, 0, 0))
    row_spec = pl.BlockSpec((1, h, 1, dv), lambda i: (i, 0, 0, 0))
    state_spec = pl.BlockSpec((1, h, dk, dv), lambda i: (i, 0, 0, 0))
    out, new_state = pl.pallas_call(
        _gla_step_kernel,
        grid=(bsz,),
        in_specs=[state_spec, col_spec, col_spec, col_spec, row_spec, row_spec, pl.BlockSpec((1, dv), lambda i: (0, 0))],
        out_specs=[row_spec, state_spec],
        out_shape=[jax.ShapeDtypeStruct((bsz, h, 1, dv), BF16), jax.ShapeDtypeStruct(state.shape, F32)],
        compiler_params=_params("parallel"),
        name="gla_step",
    )(state, col(q), col(k), col(la), row(v), row(r), gain.reshape(1, dv))
    return out.reshape(bsz, h * dv), new_state


def _rel_bucket(n):
    n = np.asarray(n, np.int64)
    nf = np.maximum(n, 1).astype(np.float64)
    large = REL_MAX_EXACT + (np.log(nf / REL_MAX_EXACT) / math.log(REL_MAX_DIST / REL_MAX_EXACT)
                             * (REL_BUCKETS - REL_MAX_EXACT)).astype(np.int64)
    return np.where(n < REL_MAX_EXACT, n, np.minimum(large, REL_BUCKETS - 1)).astype(np.int32)


def _near_tile_distances(tk, tq):
    ratio = tq // tk
    key = np.arange(tk)[:, None]
    qry = np.arange(tq)[None, :]
    return np.stack([qry - key - tk * (ratio - 1 - v) for v in range(ratio + 1)])


def _bias_tiles_kernel(table_ref, idx_ref, o_ref, *, classes):
    h = pl.program_id(0)
    far = table_ref[REL_BUCKETS - 1, h] * LOG2E
    for v, grid_v in enumerate(classes):
        for rb, row_v in enumerate(grid_v):
            for cb, cls in enumerate(row_v):
                rows, cols = pl.ds(rb * LANES, LANES), pl.ds(cb * LANES, LANES)
                if cls == "far":
                    o_ref[0, v, rows, cols] = jnp.full((LANES, LANES), far, F32)
                elif cls == "masked":
                    o_ref[0, v, rows, cols] = jnp.full((LANES, LANES), NEG_INF, F32)
                else:
                    idx = idx_ref[v, rows, cols]
                    acc = jnp.where(idx < 0, NEG_INF, 0.0).astype(F32)
                    for bkt in range(REL_BUCKETS):
                        acc = jnp.where(idx == bkt, table_ref[bkt, h] * LOG2E, acc)
                    o_ref[0, v, rows, cols] = acc


def _bias_tiles(table, tk, tq):
    assert tk >= REL_MAX_DIST and tq % tk == 0
    dist = _near_tile_distances(tk, tq)
    idx = np.where(dist >= 0, _rel_bucket(np.maximum(dist, 0)), -1).astype(np.int32)
    blocks = dist.reshape(dist.shape[0], tk // LANES, LANES, tq // LANES, LANES)
    lo, hi = blocks.min(axis=(2, 4)), blocks.max(axis=(2, 4))
    classes = [[["far" if lo[v, r, c] >= REL_MAX_DIST else "masked" if hi[v, r, c] < 0 else "mixed"
                 for c in range(lo.shape[2])] for r in range(lo.shape[1])] for v in range(lo.shape[0])]
    heads = table.shape[1]
    n_var = idx.shape[0]
    return pl.pallas_call(
        functools.partial(_bias_tiles_kernel, classes=classes),
        grid=(heads,),
        in_specs=[pl.BlockSpec(memory_space=pltpu.SMEM), pl.BlockSpec((n_var, tk, tq), lambda h: (0, 0, 0))],
        out_specs=pl.BlockSpec((1, n_var, tk, tq), lambda h: (h, 0, 0, 0)),
        out_shape=jax.ShapeDtypeStruct((heads, n_var, tk, tq), F32),
        compiler_params=_params("parallel"),
        name="bias_tiles",
    )(table, jnp.asarray(idx))


def _scalars_kernel(table_ref, onehot_ref, q1_ref, k1_ref, q2_ref, k2_ref, bias_ref, lam_ref, *, lam_init):
    bias_ref[...] = jnp.dot(onehot_ref[...], table_ref[...], precision=lax.Precision.HIGHEST,
                            preferred_element_type=F32) * LOG2E
    s1 = jnp.sum(q1_ref[...] * k1_ref[...], axis=-1, keepdims=True)
    s2 = jnp.sum(q2_ref[...] * k2_ref[...], axis=-1, keepdims=True)
    lam_ref[...] = jnp.broadcast_to(jnp.exp(s1) - jnp.exp(s2) + lam_init, lam_ref.shape)


def _scalars(table, lq1, lk1, lq2, lk2, lam_init, n_dist):
    onehot = np.zeros((n_dist, REL_BUCKETS), np.float32)
    onehot[np.arange(n_dist), _rel_bucket(np.arange(n_dist))] = 1.0
    vec = lambda a: a.reshape(1, -1)
    return pl.pallas_call(
        functools.partial(_scalars_kernel, lam_init=lam_init),
        out_shape=[jax.ShapeDtypeStruct((n_dist, table.shape[1]), F32), jax.ShapeDtypeStruct((1, LANES), F32)],
        name="bias_lambda",
    )(table, jnp.asarray(onehot), vec(lq1), vec(lk1), vec(lq2), vec(lk2))


def _diff_finalize(acc1, l1, acc2, l2, lam, gain, post_scale):
    o = acc1 / l1 - lam * (acc2 / l2)
    ms = jnp.mean(o * o, axis=-1, keepdims=True)
    return o * lax.rsqrt(ms + NORM_EPS) * gain * post_scale


def _flash_kernel(qi_ref, kj_ref, far_ref, q_ref, k_ref, vt_ref, bias_ref, lam_ref, gain_ref, o_ref,
                  m_ref, acc_ref, *, post_scale):
    h = pl.program_id(0)
    step = pl.program_id(1)
    qi, kj = qi_ref[step], kj_ref[step]
    tk, tq = k_ref.shape[0], q_ref.shape[0]
    ratio = tq // tk
    rel = kj - ratio * qi

    @pl.when(kj == 0)
    def _():
        m_ref[...] = jnp.full_like(m_ref, NEG_INF)
        acc_ref[...] = jnp.zeros_like(acc_ref)

    def update(near, shift):
        scores = []
        for c in range(2):
            s_t = lax.dot_general(k_ref[:, c * DIFF_DH:(c + 1) * DIFF_DH], q_ref[:, c * DIFF_DH:(c + 1) * DIFF_DH],
                                  NT_DIMS, preferred_element_type=F32)
            scores.append(s_t + bias_ref[0, ratio - 1 - rel] if near else s_t)
        probs, alphas = [], []
        for c in range(2):
            m_prev = m_ref[c]
            m_new = jnp.maximum(m_prev, jnp.max(scores[c], axis=0, keepdims=True) + shift)
            alphas.append(jnp.exp2(m_prev - m_new)[:1])
            m_ref[c] = m_new
            probs.append(jnp.exp2((scores[c] - (m_new[:1] - shift)).astype(BF16)))
        for c in range(2):
            acc_ref[c] = acc_ref[c] * alphas[c] + jnp.dot(vt_ref[...], probs[c], preferred_element_type=F32)

    @pl.when(rel < -1)
    def _():
        update(False, far_ref[h])

    @pl.when(rel >= -1)
    def _():
        update(True, 0.0)

    @pl.when(rel == ratio - 1)
    def _():
        lam = jnp.concatenate([lam_ref[...]] * (tq // LANES), axis=1)
        a1, a2 = acc_ref[0], acc_ref[1]
        o_t = (a1[:DIFF_DV] / a1[DIFF_DV:DIFF_DV + 1]
               - lam * (a2[:DIFF_DV] / a2[DIFF_DV:DIFF_DV + 1]))
        ms = jnp.mean(o_t * o_t, axis=0, keepdims=True)
        y_t = o_t * lax.rsqrt(ms + NORM_EPS) * gain_ref[...] * post_scale
        o_ref[...] = y_t.T.astype(o_ref.dtype)


def _diff_prompt(q, k, v_t_ones, bias_tiles, far_bias, lam, gain, post_scale):
    t = q.shape[0]
    tk, tq = bias_tiles.shape[-2:]
    ratio = tq // tk
    dva = DIFF_DV + ATT_ONES_ROWS
    pairs = [(i, j) for i in range(t // tq) for j in range(ratio * (i + 1))]
    qi_tab = jnp.asarray(np.array([p[0] for p in pairs], np.int32))
    kj_tab = jnp.asarray(np.array([p[1] for p in pairs], np.int32))
    grid_spec = pltpu.PrefetchScalarGridSpec(
        num_scalar_prefetch=3,
        grid=(DIFF_HEADS, len(pairs)),
        in_specs=[
            pl.BlockSpec((tq, DIFF_DV), lambda h, s, qi, kj, fb: (qi[s], h)),
            pl.BlockSpec((tk, DIFF_DV), lambda h, s, qi, kj, fb: (kj[s], h)),
            pl.BlockSpec((dva, tk), lambda h, s, qi, kj, fb: (h, kj[s])),
            pl.BlockSpec((1, ratio + 1, tk, tq), lambda h, s, qi, kj, fb: (h, 0, 0, 0)),
            pl.BlockSpec((1, LANES), lambda h, s, qi, kj, fb: (0, 0)),
            pl.BlockSpec((DIFF_DV, tq), lambda h, s, qi, kj, fb: (0, 0)),
        ],
        out_specs=pl.BlockSpec((tq, DIFF_DV), lambda h, s, qi, kj, fb: (qi[s], h)),
        scratch_shapes=[pltpu.VMEM((2, 8, tq), F32), pltpu.VMEM((2, dva, tq), F32)],
    )
    gain_cols = jnp.broadcast_to(gain.reshape(DIFF_DV, 1), (DIFF_DV, tq))
    return pl.pallas_call(
        functools.partial(_flash_kernel, post_scale=post_scale),
        grid_spec=grid_spec,
        out_shape=jax.ShapeDtypeStruct((t, DIFF_HEADS * DIFF_DV), BF16),
        compiler_params=_params("parallel", "arbitrary"),
        name="diff_attn_prompt",
    )(qi_tab, kj_tab, far_bias, q, k, v_t_ones, bias_tiles, lam, gain_cols)


def _decode_kernel(pt_ref, q_ref, kn_ref, vn_ref, ka_ref, va_ref, kb_ref, vb_ref, bias_far_ref, bias_last_ref,
                   bias_new_ref, lam_ref, gain_ref, o_ref, m_ref, l_ref, acc_ref, *, n_steps, post_scale):
    step = pl.program_id(1)

    @pl.when(step == 0)
    def _():
        m_ref[...] = jnp.full_like(m_ref, NEG_INF)
        l_ref[...] = jnp.zeros_like(l_ref)
        acc_ref[...] = jnp.zeros_like(acc_ref)

    q = q_ref[0] * (LOG2E * DIFF_DH ** -0.5)

    def absorb(keys_ref, values_ref, bias_ref, shift):
        n = keys_ref.shape[0]
        size = min(n, DECODE_CHUNK)
        for lo in range(0, n, size):
            rows = pl.ds(lo, size)
            prod = keys_ref[rows] * q[None]
            values = values_ref[rows]
            for c in range(2):
                s = jnp.sum(prod[:, :, c * DIFF_DH:(c + 1) * DIFF_DH], axis=-1, keepdims=True)
                if bias_ref is not None:
                    s = s + bias_ref[rows]
                m_prev = m_ref[c]
                m_new = jnp.maximum(m_prev, jnp.max(s, axis=0) + shift)
                alpha = jnp.exp2(m_prev - m_new)
                p = jnp.exp2(s - (m_new - shift)[None])
                l_ref[c] = alpha * l_ref[c] + jnp.sum(p, axis=0)
                m_ref[c] = m_new
                weights = jnp.concatenate([p] * (DIFF_DV // LANES), axis=-1)
                acc_ref[c] = (acc_ref[c] * jnp.concatenate([alpha] * (DIFF_DV // LANES), axis=-1)
                              + jnp.sum(weights * values, axis=0))

    absorb(ka_ref.at[0], va_ref.at[0], None, bias_far_ref[...])

    @pl.when(step < n_steps - 1)
    def _():
        absorb(kb_ref.at[0], vb_ref.at[0], None, bias_far_ref[...])

    @pl.when(step == n_steps - 1)
    def _():
        absorb(kb_ref.at[0], vb_ref.at[0], bias_last_ref, 0.0)
        absorb(kn_ref, vn_ref, bias_new_ref, 0.0)
        rep = lambda a: jnp.concatenate([a] * (DIFF_DV // LANES), axis=-1)
        o_ref[0] = _diff_finalize(acc_ref[0], rep(l_ref[0]), acc_ref[1], rep(l_ref[1]), rep(lam_ref[...]),
                                  gain_ref[...], post_scale).astype(o_ref.dtype)


def _diff_decode(q, k_new, v_new, cache_k, cache_v, page_table, bias_far, bias_last, bias_new, lam, gain,
                 post_scale):
    bsz, n_pages = page_table.shape
    assert n_pages % 2 == 0
    n_steps = n_pages // 2
    blk = (1, DIFF_HEADS, DIFF_DV)
    tok_spec = pl.BlockSpec(blk, lambda b, p, pt: (b, 0, 0))
    page_a = pl.BlockSpec((1, PAGE_SIZE, DIFF_HEADS, DIFF_DV), lambda b, p, pt: (pt[b, 2 * p], 0, 0, 0))
    page_b = pl.BlockSpec((1, PAGE_SIZE, DIFF_HEADS, DIFF_DV), lambda b, p, pt: (pt[b, 2 * p + 1], 0, 0, 0))
    grid_spec = pltpu.PrefetchScalarGridSpec(
        num_scalar_prefetch=1,
        grid=(bsz, n_steps),
        in_specs=[
            tok_spec, tok_spec, tok_spec, page_a, page_a, page_b, page_b,
            pl.BlockSpec((DIFF_HEADS, LANES), lambda b, p, pt: (0, 0)),
            pl.BlockSpec((PAGE_SIZE, DIFF_HEADS, LANES), lambda b, p, pt: (0, 0, 0)),
            pl.BlockSpec((1, DIFF_HEADS, LANES), lambda b, p, pt: (0, 0, 0)),
            pl.BlockSpec((1, LANES), lambda b, p, pt: (0, 0)),
            pl.BlockSpec((1, DIFF_DV), lambda b, p, pt: (0, 0)),
        ],
        out_specs=tok_spec,
        scratch_shapes=[pltpu.VMEM((2, DIFF_HEADS, LANES), F32), pltpu.VMEM((2, DIFF_HEADS, LANES), F32),
                        pltpu.VMEM((2, DIFF_HEADS, DIFF_DV), F32)],
    )
    return pl.pallas_call(
        functools.partial(_decode_kernel, n_steps=n_steps, post_scale=post_scale),
        grid_spec=grid_spec,
        out_shape=jax.ShapeDtypeStruct((bsz, DIFF_HEADS, DIFF_DV), BF16),
        compiler_params=_params("parallel", "arbitrary"),
        name="diff_attn_decode",
    )(page_table, q, k_new, v_new, cache_k, cache_v, cache_k, cache_v, bias_far, bias_last, bias_new, lam,
      gain.reshape(1, DIFF_DV))


def _mem_attn_kernel(q_ref, k_ref, v_ref, o_ref):
    outs = []
    for h in range(MEM_HEADS):
        sl = slice(h * MEM_DH, (h + 1) * MEM_DH)
        s = lax.dot_general(q_ref[0, :, sl], k_ref[0, :, sl].astype(BF16), NT_DIMS,
                            preferred_element_type=F32) * (MEM_DH ** -0.5)
        p = jnp.exp(s - jnp.max(s, axis=-1, keepdims=True))
        p = p / jnp.sum(p, axis=-1, keepdims=True)
        outs.append(jnp.dot(p.astype(BF16), v_ref[0, :, sl].astype(BF16), preferred_element_type=F32))
    o_ref[0] = jnp.concatenate(outs, axis=-1).astype(o_ref.dtype)


def _mem_attn(q, mem_k, mem_v):
    bsz, t, d = q.shape
    n_mem = mem_k.shape[1]
    bt = min(t, 512)
    return pl.pallas_call(
        _mem_attn_kernel,
        grid=(bsz, t // bt),
        in_specs=[pl.BlockSpec((1, bt, d), lambda b, i: (b, i, 0)),
                  pl.BlockSpec((1, n_mem, d), lambda b, i: (b, 0, 0)),
                  pl.BlockSpec((1, n_mem, d), lambda b, i: (b, 0, 0))],
        out_specs=pl.BlockSpec((1, bt, d), lambda b, i: (b, i, 0)),
        out_shape=jax.ShapeDtypeStruct((bsz, t, d), BF16),
        compiler_params=_params("parallel", "parallel"),
        name="mem_attn",
    )(q, mem_k, mem_v)


def _swiglu_half_step(x, norm_g, w1, w3, w2):
    h = _rmsnorm(x, norm_g, BF16)
    d_ff = w1.shape[1]
    gu = _matmul(h, [(w1, 0), (w3, 0)], d_ff, BF16, _ep_swiglu, name="ffn_up")
    return _matmul(gu, [(w2.astype(BF16), 0)], x.shape[1], F32, functools.partial(_ep_residual, scale=0.5),
                   [(x, 0)], name="ffn_down")


def kernel(x_prompt, x_sample, cache_diff_k, cache_diff_v, cache_mem_k, cache_mem_v, state_gla, page_table, mem_prompt, rel_bias_table, ffn_a_norm, ffn_a_w1, ffn_a_w3, ffn_a_w2, mix_norm, w_in, gla_w_alpha, gla_b_alpha, gla_out_norm, diff_lambda_q1, diff_lambda_k1, diff_lambda_q2, diff_lambda_k2, diff_subln, w_branch_gla, w_branch_diff, w_gate, b_gate, w_out, cross_norm, mem_norm, w_mq, w_mk, w_mv, w_mo, ffn_b_norm, ffn_b_w1, ffn_b_w3, ffn_b_w2, final_norm):
    t, d = x_prompt.shape[1], x_prompt.shape[2]
    bsz = x_sample.shape[0]
    n_mem = mem_prompt.shape[1]
    d_mem = MEM_HEADS * MEM_DH
    d_gk = gla_w_alpha.shape[1]
    d_gla = 2 * d_gk + 2 * d
    d_diff = DIFF_HEADS * DIFF_DV
    lam_init = 0.8 - 0.6 * math.exp(-0.3 * 0)
    post_scale = 1.0 - lam_init

    xp = x_prompt.reshape(t, d)
    xs = x_sample.reshape(bsz, d)

    w_in_t = w_in.T
    w_alpha_in_t = jnp.pad(w_in_t[d_gla:d_gla + GLA_RANK], ((0, LANES - GLA_RANK), (0, 0)))
    w_diff_t = w_in_t[d_gla + GLA_RANK:]
    w_alpha = jnp.pad(gla_w_alpha, ((0, LANES - GLA_RANK), (0, 0)))
    b_alpha = gla_b_alpha.reshape(1, d_gk)
    b_gate2 = b_gate.reshape(1, 2 * d)

    bias_by_dist, lam = _scalars(rel_bias_table, diff_lambda_q1, diff_lambda_k1, diff_lambda_q2, diff_lambda_k2,
                                 lam_init, 2 * PAGE_SIZE)
    bias_tiles = _bias_tiles(rel_bias_table, ATT_TK, ATT_TQ)
    far_bias = bias_by_dist[2 * PAGE_SIZE - 1]

    def mix_inputs(x, q_dtype, q_scale):
        h = _rmsnorm(x, mix_norm, BF16)
        gla_proj = _matmul(h, [(w_in_t, 0)], d_gla, F32, _ep_plain, name="proj_gla", transposed=True)
        a_low = _matmul(h, [(w_alpha_in_t, 0)], LANES, BF16, _ep_plain, name="proj_alpha", transposed=True)
        la = _matmul(a_low, [(w_alpha, 0)], d_gk, F32, _ep_log_decay, [(b_alpha, 0)], name="log_decay")
        dq = _matmul(h, [(w_diff_t, 0)], d_diff, q_dtype, functools.partial(_ep_scaled, scale=q_scale),
                     name="proj_dq", transposed=True)
        dk, dk_b = _matmul(h, [(w_diff_t, d_diff)], d_diff, (F32, BF16), _ep_twice, name="proj_dk",
                           transposed=True)
        dv, dv_b = _matmul(h, [(w_diff_t, 2 * d_diff)], d_diff, (F32, BF16), _ep_twice, name="proj_dv",
                           transposed=True)
        return h, gla_proj, la, dq, dk, dv, dk_b, dv_b

    def merge_and_rest(x, h, gla_gated, diff_normed, mem_k, mem_v):
        gla_out = _matmul(gla_gated, [(w_branch_gla, 0)], d, F32, _ep_plain, name="branch_gla")
        diff_out = _matmul(diff_normed, [(w_branch_diff, 0)], d, F32, _ep_plain, name="branch_diff")
        merged = _matmul(h, [(w_gate, 0), (w_gate, d)], d, BF16, _ep_gate,
                         [(b_gate2, 0), (b_gate2, d), (gla_out, 0), (diff_out, 0)], name="gate_merge")
        x = _matmul(merged, [(w_out, 0)], d, F32, functools.partial(_ep_residual, scale=1.0), [(x, 0)],
                    name="merge_out")
        hc = _rmsnorm(x, cross_norm, BF16)
        qm = _matmul(hc, [(w_mq, 0)], d_mem, BF16, _ep_plain, name="mem_q")
        rows = qm.shape[0] // mem_k.shape[0]
        qm = qm.reshape(mem_k.shape[0], rows, d_mem)
        pad = (-rows) % 8
        om = _mem_attn(jnp.pad(qm, ((0, 0), (0, pad), (0, 0))), mem_k, mem_v)[:, :rows].reshape(-1, d_mem)
        x = _matmul(om, [(w_mo, 0)], d, F32, functools.partial(_ep_residual, scale=1.0), [(x, 0)], name="mem_out")
        x = _swiglu_half_step(x, ffn_b_norm, ffn_b_w1, ffn_b_w3, ffn_b_w2)
        return _rmsnorm(x, final_norm, F32)

    xp = _swiglu_half_step(xp, ffn_a_norm, ffn_a_w1, ffn_a_w3, ffn_a_w2)
    hp, gla_proj_p, la_p, dq_p, dk_p, dv_p, dk_pb, dv_pb = mix_inputs(xp, BF16, LOG2E * DIFF_DH ** -0.5)
    gla_gated_p, gla_state_prompt = _gla_prompt(gla_proj_p, la_p, gla_out_norm)
    v_t_ones = jnp.concatenate([dv_pb.T.reshape(DIFF_HEADS, DIFF_DV, t),
                                jnp.ones((DIFF_HEADS, ATT_ONES_ROWS, t), BF16)], axis=1).reshape(-1, t)
    diff_normed_p = _diff_prompt(dq_p, dk_pb, v_t_ones, bias_tiles, far_bias, lam, diff_subln, post_scale)
    mem_normed = _rmsnorm(mem_prompt.reshape(n_mem, d), mem_norm, BF16)
    mem_k_p = _matmul(mem_normed, [(w_mk, 0)], d_mem, F32, _ep_plain, name="mem_k")
    mem_v_p = _matmul(mem_normed, [(w_mv, 0)], d_mem, F32, _ep_plain, name="mem_v")
    y_prompt = merge_and_rest(xp, hp, gla_gated_p, diff_normed_p, mem_k_p[None], mem_v_p[None])

    xs = _swiglu_half_step(xs, ffn_a_norm, ffn_a_w1, ffn_a_w3, ffn_a_w2)
    hs, gla_proj_s, la_s, dq_s, dk_s, dv_s, _, _ = mix_inputs(xs, F32, 1.0)
    gla_gated_s, gla_state_sample = _gla_step(
        state_gla, gla_proj_s[:, :d_gk], gla_proj_s[:, d_gk:2 * d_gk], la_s,
        gla_proj_s[:, 2 * d_gk:2 * d_gk + d], gla_proj_s[:, 2 * d_gk + d:], gla_out_norm)
    heads3 = lambda a: a.reshape(bsz, DIFF_HEADS, DIFF_DV)
    bias_far = jnp.broadcast_to(far_bias[:, None], (DIFF_HEADS, LANES))
    bias_last = jnp.broadcast_to(bias_by_dist[PAGE_SIZE:0:-1][:, :, None], (PAGE_SIZE, DIFF_HEADS, LANES))
    bias_new = jnp.broadcast_to(bias_by_dist[0][None, :, None], (1, DIFF_HEADS, LANES))
    diff_normed_s = _diff_decode(heads3(dq_s), heads3(dk_s), heads3(dv_s), cache_diff_k, cache_diff_v, page_table,
                                 bias_far, bias_last, bias_new, lam, diff_subln, post_scale)
    y_sample = merge_and_rest(xs, hs, gla_gated_s, diff_normed_s.reshape(bsz, d_diff),
                              cache_mem_k.reshape(bsz, n_mem, d_mem), cache_mem_v.reshape(bsz, n_mem, d_mem))

    return (y_prompt.reshape(1, t, d), y_sample.reshape(bsz, 1, d),
            dk_p.reshape(1, t, DIFF_HEADS, DIFF_DV), dv_p.reshape(1, t, DIFF_HEADS, DIFF_DV),
            dk_s.reshape(bsz, 1, DIFF_HEADS, DIFF_DV), dv_s.reshape(bsz, 1, DIFF_HEADS, DIFF_DV),
            gla_state_prompt[None], gla_state_sample,
            mem_k_p.reshape(1, n_mem, MEM_HEADS, MEM_DH), mem_v_p.reshape(1, n_mem, MEM_HEADS, MEM_DH))
```
